```python
import jax
import jax.numpy as jnp
from jax import lax
import numpy as np

D_MODEL = 2048
BATCH = 1
SEQ = 8192
DEPTH = 4

GRID_W = 64
RET_HEADS = 4
RET_HEAD_DIM = 256
RET_WIDTH = RET_HEADS * RET_HEAD_DIM
RET_CHUNK = 128
NA_HEADS = 8
NA_HEAD_DIM = 128
NA_WIDTH = NA_HEADS * NA_HEAD_DIM
NA_KH = 8
NA_KW = 16
LRU_WIDTH = 1024
LRU_BLOCKS = 8
LRU_BLOCK_DIM = LRU_WIDTH // LRU_BLOCKS
LRU_CONV = 4
LRU_C = 8.0
N_BRANCH = 3
BRANCH_WIDTH = 1024
IN_COLS = 4 * RET_WIDTH + 3 * NA_WIDTH + 2 * LRU_WIDTH + N_BRANCH * D_MODEL
D_FF = -(-8 * D_MODEL // (3 * 256)) * 256
DEEPNORM_ALPHA = (2 * DEPTH) ** 0.25
DEEPNORM_BETA = (8 * DEPTH) ** -0.25
LN_EPS = 1e-5
ROPE_BASE = 10000.0

kernel_name = 'hybrid_retention_natten_rglru_encoder'


def layer_norm(x, g, b):
    xf = x.astype(jnp.float32)
    mu = jnp.mean(xf, axis=-1, keepdims=True)
    var = jnp.mean(jnp.square(xf - mu), axis=-1, keepdims=True)
    y = (xf - mu) * lax.rsqrt(var + LN_EPS)
    return (y * g.astype(jnp.float32) + b.astype(jnp.float32)).astype(x.dtype)


def split_columns(proj):
    sizes = [RET_WIDTH] * 4 + [NA_WIDTH] * 3 + [LRU_WIDTH] * 2 + [N_BRANCH * D_MODEL]
    parts = []
    start = 0
    for w in sizes:
        parts.append(proj[..., start:start + w])
        start += w
    return parts


def rotary(t, pos):
    half = t.shape[-1] // 2
    inv = ROPE_BASE ** (-jnp.arange(half, dtype=jnp.float32) / half)
    ang = pos[:, None] * inv[None, :]
    cos = jnp.cos(ang)[None, :, None, :]
    sin = jnp.sin(ang)[None, :, None, :]
    t1 = t[..., :half].astype(jnp.float32)
    t2 = t[..., half:].astype(jnp.float32)
    return jnp.concatenate([t1 * cos - t2 * sin, t1 * sin + t2 * cos], axis=-1)


def decay_masks(log_g, strict):
    idx = jnp.arange(RET_CHUNK, dtype=jnp.float32)
    diff = idx[:, None] - idx[None, :]
    keep = (diff > 0) if strict else (diff >= 0)
    expo = jnp.where(keep, diff, 0.0)[None] * log_g[:, None, None]
    inner = jnp.where(keep[None], jnp.exp(expo), 0.0)
    q_dec = jnp.exp((idx + 1.0)[None, :] * log_g[:, None])
    k_dec = jnp.exp((RET_CHUNK - 1.0 - idx)[None, :] * log_g[:, None])
    c_dec = jnp.exp(RET_CHUNK * log_g)
    return inner, q_dec, k_dec, c_dec


def retention_one_direction(q, k, v, log_g, strict):
    b, h, s, d = q.shape
    nc = s // RET_CHUNK
    q = q.reshape(b, h, nc, RET_CHUNK, d)
    k = k.reshape(b, h, nc, RET_CHUNK, d)
    v = v.reshape(b, h, nc, RET_CHUNK, v.shape[-1])
    inner, q_dec, k_dec, c_dec = decay_masks(log_g, strict)
    scores = jnp.einsum('bhncd,bhnmd->bhncm', q, k) * inner[None, :, None]
    intra = jnp.einsum('bhncm,bhnme->bhnce', scores, v)
    kv = jnp.einsum('bhncd,bhnce->nbhde', k * k_dec[None, :, None, :, None], v)
    decay = c_dec[None, :, None, None]

    def step(state, kv_n):
        return decay * state + kv_n, state

    _, prev = lax.scan(step, jnp.zeros_like(kv[0]), kv)
    inter = jnp.einsum('bhncd,nbhde->bhnce', q * q_dec[None, :, None, :, None], prev)
    return (intra + inter).reshape(b, h, s, -1)


def retention_branch(q, k, v, g, decay_logits):
    b, s, _ = q.shape
    pos = jnp.arange(s, dtype=jnp.float32)
    heads = lambda t: t.reshape(b, s, RET_HEADS, RET_HEAD_DIM)
    to_bhsd = lambda t: jnp.transpose(t, (0, 2, 1, 3)).astype(jnp.float32)
    qh = to_bhsd(rotary(heads(q), pos))
    kh = to_bhsd(rotary(heads(k), pos) * (RET_HEAD_DIM ** -0.5))
    vh = to_bhsd(heads(v))
    log_g = jax.nn.log_sigmoid(decay_logits.astype(jnp.float32))
    flip = lambda t: jnp.flip(t, axis=2)
    fwd = retention_one_direction(qh, kh, vh, log_g[0], strict=False)
    bwd = flip(retention_one_direction(flip(qh), flip(kh), flip(vh), log_g[1], strict=True))
    y = fwd + bwd
    mu = jnp.mean(y, axis=-1, keepdims=True)
    var = jnp.mean(jnp.square(y - mu), axis=-1, keepdims=True)
    y = (y - mu) * lax.rsqrt(var + LN_EPS)
    y = jnp.transpose(y, (0, 2, 1, 3)).reshape(b, s, RET_WIDTH)
    return (jax.nn.silu(g.astype(jnp.float32)) * y).astype(g.dtype)


def neighborhood_attention_branch(q, k, v, rpb):
    b, s, _ = q.shape
    rows = s // GRID_W
    kh = min(NA_KH, rows)
    kw = NA_KW
    grid = lambda t: jnp.transpose(t.reshape(b, rows, GRID_W, NA_HEADS, NA_HEAD_DIM), (0, 3, 1, 2, 4))
    qg, kg, vg = grid(q), grid(k), grid(v)
    r = jnp.arange(rows)
    row_start = jnp.clip(r - kh // 2, 0, rows - kh)
    key_rows = row_start[:, None] + jnp.arange(kh)[None, :]
    k_strip = kg[:, :, key_rows]
    v_strip = vg[:, :, key_rows]
    c = jnp.arange(GRID_W)
    col_start = jnp.clip(c - kw // 2, 0, GRID_W - kw)
    in_win = (c[None, :] >= col_start[:, None]) & (c[None, :] < col_start[:, None] + kw)
    dr = key_rows - r[:, None] + (NA_KH - 1)
    dc = jnp.clip(c[None, :] - c[:, None], -(kw - 1), kw - 1) + (kw - 1)
    bias = rpb.astype(jnp.float32)[:, dr[:, None, :, None], dc[None, :, None, :]]
    scores = jnp.einsum('bhrcd,bhrkwd->bhrckw', qg, k_strip).astype(jnp.float32) * (NA_HEAD_DIM ** -0.5)
    scores = jnp.where(in_win[:, None, :], scores + bias[None], -jnp.inf)
    p = jax.nn.softmax(scores, axis=(-2, -1)).astype(v.dtype)
    out = jnp.einsum('bhrckw,bhrkwd->bhrcd', p, v_strip)
    return jnp.transpose(out, (0, 2, 3, 1, 4)).reshape(b, s, NA_WIDTH)


def linear_combine(left, right):
    a_l, b_l = left
    a_r, b_r = right
    return a_l * a_r, a_r * b_l + b_r


def rglru_branch(xb, yb, w_conv, b_conv, wa, ba, wi, bi, lam):
    b, s, _ = xb.shape
    xc = lax.conv_general_dilated(
        xb, w_conv.astype(xb.dtype)[:, None, :], window_strides=(1,),
        padding=[(LRU_CONV // 2, LRU_CONV - 1 - LRU_CONV // 2)],
        dimension_numbers=('NWC', 'WIO', 'NWC'), feature_group_count=LRU_WIDTH) + b_conv
    xf = xc.astype(jnp.float32)
    xblk = xf.reshape(b, s, LRU_BLOCKS, LRU_BLOCK_DIM)

    def direction(d, reverse):
        r_gate = jax.nn.sigmoid(jnp.einsum('bsni,nij->bsnj', xblk, wa[d].astype(jnp.float32)).reshape(b, s, LRU_WIDTH)
                                + ba[d].astype(jnp.float32))
        i_gate = jax.nn.sigmoid(jnp.einsum('bsni,nij->bsnj', xblk, wi[d].astype(jnp.float32)).reshape(b, s, LRU_WIDTH)
                                + bi[d].astype(jnp.float32))
        log_a = -LRU_C * r_gate * jax.nn.softplus(-lam[d].astype(jnp.float32))
        a = jnp.exp(log_a)
        inp = jnp.sqrt(-jnp.expm1(2.0 * log_a)) * (i_gate * xf)
        _, h = lax.associative_scan(linear_combine, (a, inp), axis=1, reverse=reverse)
        return h

    h = direction(0, False) + direction(1, True)
    return (h * jax.nn.gelu(yb.astype(jnp.float32))).astype(xb.dtype)


def hybrid_mixer(x, w_in, gate_b, ret_decay, w_conv, b_conv, lru_wa, lru_ba, lru_wi, lru_bi,
                 lru_lambda, na_rpb, w_branch, w_out):
    b, s, _ = x.shape
    proj = jnp.einsum('bsd,dc->bsc', x, w_in)
    rq, rk, rv, rg, nq, nk, nv, lx, ly, gate_pre = split_columns(proj)
    ret = retention_branch(rq, rk, rv, rg, ret_decay)
    na = neighborhood_attention_branch(nq, nk, nv, na_rpb)
    lru = rglru_branch(lx, ly, w_conv, b_conv, lru_wa, lru_ba, lru_wi, lru_bi, lru_lambda)
    branches = jnp.stack([ret, na, lru], axis=2)
    up = jnp.einsum('bsni,nid->bsnd', branches, w_branch)
    gates = jax.nn.sigmoid(gate_pre + gate_b).reshape(b, s, N_BRANCH, D_MODEL)
    merged = jnp.sum(gates * up, axis=2)
    return jnp.einsum('bsd,de->bse', merged, w_out)


def swiglu(x, w_ffn_in, w_ffn_out):
    h = jnp.einsum('bsd,df->bsf', x, w_ffn_in)
    gate, val = h[..., :D_FF], h[..., D_FF:]
    return jnp.einsum('bsf,fd->bsd', jax.nn.silu(gate) * val, w_ffn_out)


def setup_inputs(seed: int = 0) -> dict:
    key = jax.random.key(seed)
    ks = jax.random.split(key, 22)
    nrm = lambda k, shape, scale: scale * jax.random.normal(k, shape, jnp.float32)
    gamma0 = 1.0 - 2.0 ** (-5.0 - jnp.arange(RET_HEADS, dtype=jnp.float32))
    decay_logit = jnp.log(gamma0) - jnp.log1p(-gamma0)
    a8 = jax.random.uniform(ks[12], (DEPTH, 2, LRU_WIDTH), jnp.float32, 0.9, 0.999)
    a = a8 ** (1.0 / LRU_C)
    return {
        'x': nrm(ks[0], (BATCH, SEQ, D_MODEL), 1.0),
        'ln_in_g': 1.0 + nrm(ks[1], (D_MODEL,), 0.02),
        'ln_in_b': nrm(ks[2], (D_MODEL,), 0.02),
        'w_in': nrm(ks[3], (DEPTH, D_MODEL, IN_COLS), D_MODEL ** -0.5),
        'gate_b': nrm(ks[4], (DEPTH, N_BRANCH * D_MODEL), 0.1),
        'ret_decay': decay_logit + nrm(ks[5], (DEPTH, 2, RET_HEADS), 0.05),
        'w_conv': nrm(ks[6], (DEPTH, LRU_CONV, LRU_WIDTH), LRU_CONV ** -0.5),
        'b_conv': nrm(ks[7], (DEPTH, LRU_WIDTH), 0.02),
        'lru_wa': nrm(ks[8], (DEPTH, 2, LRU_BLOCKS, LRU_BLOCK_DIM, LRU_BLOCK_DIM), LRU_BLOCK_DIM ** -0.5),
        'lru_ba': nrm(ks[9], (DEPTH, 2, LRU_WIDTH), 0.02),
        'lru_wi': nrm(ks[10], (DEPTH, 2, LRU_BLOCKS, LRU_BLOCK_DIM, LRU_BLOCK_DIM), LRU_BLOCK_DIM ** -0.5),
        'lru_bi': nrm(ks[11], (DEPTH, 2, LRU_WIDTH), 0.02),
        'lru_lambda': jnp.log(a) - jnp.log1p(-a),
        'na_rpb': nrm(ks[13], (DEPTH, NA_HEADS, 2 * NA_KH - 1, 2 * NA_KW - 1), 0.1),
        'w_branch': nrm(ks[14], (DEPTH, N_BRANCH, BRANCH_WIDTH, D_MODEL), BRANCH_WIDTH ** -0.5),
        'w_out': nrm(ks[15], (DEPTH, D_MODEL, D_MODEL), DEEPNORM_BETA * D_MODEL ** -0.5),
        'ln1_g': 1.0 + nrm(ks[16], (DEPTH, D_MODEL), 0.02),
        'ln1_b': nrm(ks[17], (DEPTH, D_MODEL), 0.02),
        'w_ffn_in': nrm(ks[18], (DEPTH, D_MODEL, 2 * D_FF), D_MODEL ** -0.5),
        'w_ffn_out': nrm(ks[19], (DEPTH, D_FF, D_MODEL), DEEPNORM_BETA * D_FF ** -0.5),
        'ln2_g': 1.0 + nrm(ks[20], (DEPTH, D_MODEL), 0.02),
        'ln2_b': nrm(ks[21], (DEPTH, D_MODEL), 0.02),
    }


def reference(x, ln_in_g, ln_in_b, w_in, gate_b, ret_decay, w_conv, b_conv, lru_wa, lru_ba,
              lru_wi, lru_bi, lru_lambda, na_rpb, w_branch, w_out, ln1_g, ln1_b,
              w_ffn_in, w_ffn_out, ln2_g, ln2_b):
    h = layer_norm(x, ln_in_g, ln_in_b)
    for l in range(DEPTH):
        mix = hybrid_mixer(h, w_in[l], gate_b[l], ret_decay[l], w_conv[l], b_conv[l],
                           lru_wa[l], lru_ba[l], lru_wi[l], lru_bi[l], lru_lambda[l],
                           na_rpb[l], w_branch[l], w_out[l])
        h = layer_norm(DEEPNORM_ALPHA * h + mix, ln1_g[l], ln1_b[l])
        h = layer_norm(DEEPNORM_ALPHA * h + swiglu(h, w_ffn_in[l], w_ffn_out[l]), ln2_g[l], ln2_b[l])
    return h
```

```python
import functools
import math

import jax
import jax.numpy as jnp
from jax import lax
from jax.experimental import pallas as pl
from jax.experimental.pallas import tpu as pltpu

F32 = jnp.float32
BF16 = jnp.bfloat16

D_MODEL = 2048
DEPTH = 4
GRID_W = 64
RET_HEADS = 4
RET_HEAD_DIM = 256
RET_CHUNK = 128
NA_HEADS = 8
NA_HEAD_DIM = 128
NA_KH = 8
NA_KW = 16
LRU_WIDTH = 1024
LRU_BLOCKS = 8
LRU_BLOCK_DIM = 128
LRU_CONV = 4
LRU_C = 8.0
N_BRANCH = 3
BRANCH_WIDTH = 1024
IN_COLS = 15360
D_FF = 5632
DEEPNORM_ALPHA = (2 * DEPTH) ** 0.25
LN_EPS = 1e-5
ROPE_BASE = 10000.0

COL_RQ, COL_RK, COL_RV, COL_RG = 0, 1024, 2048, 3072
COL_NQ, COL_NK, COL_NV = 4096, 5120, 6144
COL_LX, COL_LY, COL_GATE = 7168, 8192, 9216

NA_ROWS_PER_STEP = 8
NA_KEY_ROWS = 16
NA_NEG = -1e30

LRU_TILE = 256
LRU_HALO = 8

VMEM_LIMIT = 56 * 1024 * 1024


def _params(sem, vmem=VMEM_LIMIT):
    return pltpu.CompilerParams(dimension_semantics=sem, vmem_limit_bytes=vmem)


def _softplus(x):
    return jnp.maximum(x, 0.0) + jnp.log1p(jnp.exp(-jnp.abs(x)))


def _log_sigmoid(x):
    return -_softplus(-x)


def _layer_norm_rows(x, g, b):
    mu = jnp.mean(x, axis=-1, keepdims=True)
    xc = x - mu
    var = jnp.mean(xc * xc, axis=-1, keepdims=True)
    return xc * lax.rsqrt(var + LN_EPS) * g + b


def _ln_kernel(x_ref, g_ref, b_ref, o32_ref, o16_ref):
    y = _layer_norm_rows(x_ref[...], g_ref[...], b_ref[...])
    o32_ref[...] = y
    o16_ref[...] = y.astype(BF16)


def _ln(x, g, b, bm=512):
    s, d = x.shape
    row = pl.BlockSpec((bm, d), lambda i: (i, 0))
    vec = pl.BlockSpec((1, d), lambda i: (0, 0))
    return pl.pallas_call(
        _ln_kernel,
        grid=(s // bm,),
        in_specs=[row, vec, vec],
        out_specs=[row, row],
        out_shape=[jax.ShapeDtypeStruct((s, d), F32), jax.ShapeDtypeStruct((s, d), BF16)],
        compiler_params=_params(("parallel",)),
        name="ln_in",
    )(x, g.reshape(1, d), b.reshape(1, d))


def _mm_kernel(x_ref, w_ref, o_ref):
    o_ref[...] = jnp.dot(x_ref[...], w_ref[...], preferred_element_type=F32).astype(o_ref.dtype)


def _matmul(x16, w16, layer, out_dtype, bm=1024, bn=1024):
    s, k = x16.shape
    n = w16.shape[2]
    return pl.pallas_call(
        _mm_kernel,
        grid=(n // bn, s // bm),
        in_specs=[pl.BlockSpec((bm, k), lambda j, i: (i, 0)),
                  pl.BlockSpec((None, k, bn), lambda j, i: (layer, 0, j))],
        out_specs=pl.BlockSpec((bm, bn), lambda j, i: (i, j)),
        out_shape=jax.ShapeDtypeStruct((s, n), out_dtype),
        compiler_params=_params(("parallel", "parallel")),
        name="in_proj",
    )(x16, w16)


def _rope_kernel(q_ref, k_ref, cos_ref, sin_ref, qo_ref, ko_ref):
    c = cos_ref[...]
    s = sin_ref[...]
    half = RET_HEAD_DIM // 2
    k_scale = RET_HEAD_DIM ** -0.5
    for h in range(RET_HEADS):
        lo = slice(h * RET_HEAD_DIM, h * RET_HEAD_DIM + half)
        hi = slice(h * RET_HEAD_DIM + half, (h + 1) * RET_HEAD_DIM)
        q1, q2 = q_ref[:, lo], q_ref[:, hi]
        qo_ref[:, lo] = (q1 * c - q2 * s).astype(BF16)
        qo_ref[:, hi] = (q1 * s + q2 * c).astype(BF16)
        k1, k2 = k_ref[:, lo], k_ref[:, hi]
        ko_ref[:, lo] = ((k1 * c - k2 * s) * k_scale).astype(BF16)
        ko_ref[:, hi] = ((k1 * s + k2 * c) * k_scale).astype(BF16)


def _rope(proj, cos, sin, bm=512):
    s = proj.shape[0]
    w = RET_HEADS * RET_HEAD_DIM
    half = RET_HEAD_DIM // 2
    tab = pl.BlockSpec((bm, half), lambda i: (i, 0))
    out = pl.BlockSpec((bm, w), lambda i: (i, 0))
    return pl.pallas_call(
        _rope_kernel,
        grid=(s // bm,),
        in_specs=[pl.BlockSpec((bm, w), lambda i: (i, COL_RQ // w)),
                  pl.BlockSpec((bm, w), lambda i: (i, COL_RK // w)),
                  tab, tab],
        out_specs=[out, out],
        out_shape=[jax.ShapeDtypeStruct((s, w), BF16)] * 2,
        compiler_params=_params(("parallel",)),
        name="rope",
    )(proj, proj, cos, sin)


def _chunk_pos(shape):
    return lax.broadcasted_iota(jnp.int32, shape, 0).astype(F32)


def _ret_bwd_kernel(q_ref, k_ref, v_ref, dl_ref, o_ref, s_ref):
    c = RET_CHUNK

    @pl.when(pl.program_id(1) == 0)
    def _():
        s_ref[...] = jnp.zeros_like(s_ref)

    lg = _log_sigmoid(dl_ref[...])
    pos = _chunk_pos((c, RET_HEAD_DIM))
    q_dec = jnp.exp((c - pos) * lg)
    k_dec = jnp.exp(pos * lg)
    c_dec = jnp.exp(c * lg)
    state = s_ref[...]
    inter = jnp.dot(q_ref[...], state.astype(BF16), preferred_element_type=F32)
    o_ref[...] = inter * q_dec
    vk = (v_ref[...] * k_dec).astype(BF16)
    kv = lax.dot_general(k_ref[...], vk, (((0,), (0,)), ((), ())), preferred_element_type=F32)
    s_ref[...] = c_dec * state + kv


def _ret_fwd_kernel(q_ref, k_ref, v_ref, g_ref, ib_ref, dlf_ref, dlb_ref, o_ref, s_ref):
    c = RET_CHUNK

    @pl.when(pl.program_id(1) == 0)
    def _():
        s_ref[...] = jnp.zeros_like(s_ref)

    lgf = _log_sigmoid(dlf_ref[...])
    lgb = _log_sigmoid(dlb_ref[...])
    pos = _chunk_pos((c, RET_HEAD_DIM))
    q_dec = jnp.exp((pos + 1.0) * lgf)
    k_dec = jnp.exp((c - 1.0 - pos) * lgf)
    c_dec = jnp.exp(c * lgf)
    ri = lax.broadcasted_iota(jnp.int32, (c, c), 0)
    ci = lax.broadcasted_iota(jnp.int32, (c, c), 1)
    diff = (ri - ci).astype(F32)
    mask = jnp.exp(jnp.abs(diff) * jnp.where(diff >= 0, lgf[:, :c], lgb[:, :c]))

    q = q_ref[...]
    k = k_ref[...]
    v = v_ref[...]
    scores = lax.dot_general(q, k, (((1,), (1,)), ((), ())), preferred_element_type=F32)
    p = (scores * mask).astype(BF16)
    intra = jnp.dot(p, v.astype(BF16), preferred_element_type=F32)
    state = s_ref[...]
    inter = jnp.dot(q, state.astype(BF16), preferred_element_type=F32) * q_dec
    y = intra + inter + ib_ref[...]

    vk = (v * k_dec).astype(BF16)
    kv = lax.dot_general(k, vk, (((0,), (0,)), ((), ())), preferred_element_type=F32)
    s_ref[...] = c_dec * state + kv

    mu = jnp.mean(y, axis=-1, keepdims=True)
    yc = y - mu
    var = jnp.mean(yc * yc, axis=-1, keepdims=True)
    yn = yc * lax.rsqrt(var + LN_EPS)
    g = g_ref[...]
    o_ref[...] = (g * jax.nn.sigmoid(g) * yn).astype(BF16)


def _retention(q16, k16, proj, decay_lanes):
    s = q16.shape[0]
    c, d, nh = RET_CHUNK, RET_HEAD_DIM, RET_HEADS
    nc = s // c
    w = nh * d
    sem = ("parallel", "arbitrary")

    def rev(h, n):
        return (nc - 1 - n, h)

    inter_b = pl.pallas_call(
        _ret_bwd_kernel,
        grid=(nh, nc),
        in_specs=[pl.BlockSpec((c, d), rev),
                  pl.BlockSpec((c, d), rev),
                  pl.BlockSpec((c, d), lambda h, n: (nc - 1 - n, COL_RV // d + h)),
                  pl.BlockSpec((None, 1, d), lambda h, n: (nh + h, 0, 0))],
        out_specs=pl.BlockSpec((c, d), rev),
        out_shape=jax.ShapeDtypeStruct((s, w), F32),
        scratch_shapes=[pltpu.VMEM((d, d), F32)],
        compiler_params=_params(sem),
        name="ret_bwd",
    )(q16, k16, proj, decay_lanes)

    def fwd(h, n):
        return (n, h)

    return pl.pallas_call(
        _ret_fwd_kernel,
        grid=(nh, nc),
        in_specs=[pl.BlockSpec((c, d), fwd),
                  pl.BlockSpec((c, d), fwd),
                  pl.BlockSpec((c, d), lambda h, n: (n, COL_RV // d + h)),
                  pl.BlockSpec((c, d), lambda h, n: (n, COL_RG // d + h)),
                  pl.BlockSpec((c, d), fwd),
                  pl.BlockSpec((None, 1, d), lambda h, n: (h, 0, 0)),
                  pl.BlockSpec((None, 1, d), lambda h, n: (nh + h, 0, 0))],
        out_specs=pl.BlockSpec((c, d), fwd),
        out_shape=jax.ShapeDtypeStruct((s, w), BF16),
        scratch_shapes=[pltpu.VMEM((d, d), F32)],
        compiler_params=_params(sem),
        name="ret_fwd",
    )(q16, k16, proj, proj, inter_b, decay_lanes, decay_lanes)


def _na_build_bias(rpb_ref, bias_ref, head):
    w = GRID_W
    n_dc = 2 * NA_KW - 1
    n_dr = 2 * NA_KH - 1
    lane = lax.broadcasted_iota(jnp.int32, (w, 2 * w), 1)
    cq = lax.broadcasted_iota(jnp.int32, (w, 2 * w), 0)
    ck = lane & (w - 1)
    dc = jnp.clip(ck - cq, -(NA_KW - 1), NA_KW - 1) + (NA_KW - 1)
    col_start = jnp.clip(cq - NA_KW // 2, 0, w - NA_KW)
    in_win = (ck >= col_start) & (ck < col_start + NA_KW)
    dc_is = [dc == e for e in range(n_dc)]
    neg = jnp.full((w, 2 * w), NA_NEG, F32)
    rows = []
    for dr in range(n_dr):
        t = jnp.zeros((w, 2 * w), F32)
        base = (head * n_dr + dr) * n_dc
        for e in range(n_dc):
            t = jnp.where(dc_is[e], rpb_ref[base + e], t)
        rows.append(jnp.where(in_win, t, neg))
    left = lane < w
    span = NA_KEY_ROWS - NA_KH
    for case, off in enumerate((0, span // 2, span)):
        for i in range(NA_ROWS_PER_STEP):
            first = (max(i - NA_KH // 2, 0), i, min(i + NA_KH // 2, span))[case]

            def tile(j):
                if first <= j < first + NA_KH:
                    return rows[j - i + (NA_KH - 1) - off]
                return None

            for jp in range(NA_KEY_ROWS // 2):
                a, b = tile(2 * jp), tile(2 * jp + 1)
                if a is None and b is None:
                    blk = neg
                else:
                    blk = jnp.where(left, neg if a is None else a, neg if b is None else b)
                bias_ref[case, i * w:(i + 1) * w, jp * 2 * w:(jp + 1) * 2 * w] = blk


def _na_kernel(rpb_ref, q_ref, k_ref, v_ref, o_ref, kb_ref, vb_ref, bias_ref, *, rows):
    head = pl.program_id(0)
    rb = pl.program_id(1)
    nrb = rows // NA_ROWS_PER_STEP
    nkeys = NA_KEY_ROWS * GRID_W

    @pl.when(rb == 0)
    def _():
        kb_ref[...] = k_ref[...].astype(BF16)
        vb_ref[...] = v_ref[...].astype(BF16)
        _na_build_bias(rpb_ref, bias_ref, head)

    key_row0 = jnp.clip(rb * NA_ROWS_PER_STEP - NA_KH // 2, 0, rows - NA_KEY_ROWS)
    kstart = pl.multiple_of(key_row0 * GRID_W, (NA_KH // 2) * GRID_W)
    kblk = kb_ref[pl.ds(kstart, nkeys), :]
    vblk = vb_ref[pl.ds(kstart, nkeys), :]
    case = jnp.where(rb == 0, 0, jnp.where(rb == nrb - 1, 2, 1))
    q = q_ref[...].astype(BF16)
    s = lax.dot_general(q, kblk, (((1,), (1,)), ((), ())), preferred_element_type=F32)
    s = s * (NA_HEAD_DIM ** -0.5) + bias_ref[case]
    m = jnp.max(s, axis=-1, keepdims=True)
    e = jnp.exp(s - m)
    l = jnp.sum(e, axis=-1, keepdims=True)
    o = jnp.dot(e.astype(BF16), vblk, preferred_element_type=F32) / l
    o_ref[...] = o.astype(BF16)


def _neighborhood_attention(proj, rpb_flat):
    s = proj.shape[0]
    rows = s // GRID_W
    d = NA_HEAD_DIM
    bq = NA_ROWS_PER_STEP * GRID_W
    nrb = rows // NA_ROWS_PER_STEP
    return pl.pallas_call(
        functools.partial(_na_kernel, rows=rows),
        grid=(NA_HEADS, nrb),
        in_specs=[pl.BlockSpec(memory_space=pltpu.SMEM),
                  pl.BlockSpec((bq, d), lambda h, r: (r, COL_NQ // d + h)),
                  pl.BlockSpec((s, d), lambda h, r: (0, COL_NK // d + h)),
                  pl.BlockSpec((s, d), lambda h, r: (0, COL_NV // d + h))],
        out_specs=pl.BlockSpec((bq, d), lambda h, r: (r, h)),
        out_shape=jax.ShapeDtypeStruct((s, NA_HEADS * d), BF16),
        scratch_shapes=[pltpu.VMEM((s, d), BF16), pltpu.VMEM((s, d), BF16),
                        pltpu.VMEM((3, bq, NA_KEY_ROWS * GRID_W), F32)],
        compiler_params=_params(("parallel", "arbitrary")),
        name="natten",
    )(rpb_flat, proj, proj, proj)


def _gelu_tanh(x):
    return x * (0.5 * (1.0 + jnp.tanh(math.sqrt(2.0 / math.pi) * (x + 0.044715 * (x * x * x)))))


def _lru_kernel(x_ref, y_ref, wc_ref, bc_ref, waf_ref, wif_ref, wab_ref, wib_ref, ba_ref, bi_ref,
                lam_ref, o_ref, xpad_ref, hf_ref, a_ref, b_ref, hb_ref, *, seq):
    tt = LRU_TILE
    nt = seq // tt
    ng = tt // 8
    lanes = LRU_BLOCK_DIM
    row = lax.broadcasted_iota(jnp.int32, (8, lanes), 0)

    zeros_halo = jnp.zeros((LRU_HALO, lanes), F32)
    xpad_ref[0:LRU_HALO, :] = zeros_halo
    xpad_ref[LRU_HALO + seq:2 * LRU_HALO + seq, :] = zeros_halo

    def stage(t, carry):
        t0 = pl.multiple_of(t * tt, tt)
        xpad_ref[pl.ds(LRU_HALO + t0, tt), :] = x_ref[pl.ds(t0, tt), :]
        return carry

    lax.fori_loop(0, nt, stage, 0)

    def tile_gates(t0, wa_ref, wi_ref, d):
        base = LRU_HALO - LRU_CONV // 2
        xc = bc_ref[...]
        for j in range(LRU_CONV):
            xc = xc + xpad_ref[pl.ds(t0 + base + j, tt), :] * wc_ref[j:j + 1, :]
        xb = xc.astype(BF16)
        r = jax.nn.sigmoid(jnp.dot(xb, wa_ref[...].astype(BF16), preferred_element_type=F32)
                           + ba_ref[d:d + 1, :])
        i = jax.nn.sigmoid(jnp.dot(xb, wi_ref[...].astype(BF16), preferred_element_type=F32)
                           + bi_ref[d:d + 1, :])
        log_a = (-LRU_C) * r * _softplus(-lam_ref[d:d + 1, :])
        a = jnp.exp(log_a)
        a_ref[...] = a
        b_ref[...] = jnp.sqrt(-jnp.tanh(log_a) * (a * a + 1.0)) * (i * xc)

    def fwd_tile(t, carry):
        t0 = pl.multiple_of(t * tt, tt)
        tile_gates(t0, waf_ref, wif_ref, 0)

        def group(gi, c):
            r0 = pl.multiple_of(gi * 8, 8)
            a = a_ref[pl.ds(r0, 8), :]
            b = b_ref[pl.ds(r0, 8), :]
            for sh in (1, 2, 4):
                keep = row >= sh
                a_sh = jnp.where(keep, pltpu.roll(a, sh, 0), 1.0)
                b_sh = jnp.where(keep, pltpu.roll(b, sh, 0), 0.0)
                b = a * b_sh + b
                a = a * a_sh
            h = a * c + b
            hf_ref[pl.ds(t0 + r0, 8), :] = h
            return jnp.broadcast_to(h[7:8, :], (8, lanes))

        return lax.fori_loop(0, ng, group, carry, unroll=8)

    lax.fori_loop(0, nt, fwd_tile, jnp.zeros((8, lanes), F32))

    def bwd_tile(ti, carry):
        t0 = pl.multiple_of((nt - 1 - ti) * tt, tt)
        tile_gates(t0, wab_ref, wib_ref, 1)

        def group(gj, c):
            r0 = pl.multiple_of((ng - 1 - gj) * 8, 8)
            a = a_ref[pl.ds(r0, 8), :]
            b = b_ref[pl.ds(r0, 8), :]
            for sh in (1, 2, 4):
                keep = row < 8 - sh
                a_sh = jnp.where(keep, pltpu.roll(a, 8 - sh, 0), 1.0)
                b_sh = jnp.where(keep, pltpu.roll(b, 8 - sh, 0), 0.0)
                b = a * b_sh + b
                a = a * a_sh
            h = a * c + b
            hb_ref[pl.ds(r0, 8), :] = h
            return jnp.broadcast_to(h[0:1, :], (8, lanes))

        carry = lax.fori_loop(0, ng, group, carry, unroll=8)
        h = hf_ref[pl.ds(t0, tt), :] + hb_ref[...]
        o_ref[pl.ds(t0, tt), :] = (h * _gelu_tanh(y_ref[pl.ds(t0, tt), :])).astype(BF16)
        return carry

    lax.fori_loop(0, nt, bwd_tile, jnp.zeros((8, lanes), F32))


def _rglru(proj, w_conv, b_conv, wa, ba, wi, bi, lam):
    s = proj.shape[0]
    bd = LRU_BLOCK_DIM
    col = lambda off: pl.BlockSpec((s, bd), lambda g: (0, off // bd + g))
    gate_w = lambda d: pl.BlockSpec((None, None, bd, bd), lambda g: (d, g, 0, 0))
    vec2 = pl.BlockSpec((2, bd), lambda g: (0, g))
    return pl.pallas_call(
        functools.partial(_lru_kernel, seq=s),
        grid=(LRU_BLOCKS,),
        in_specs=[col(COL_LX), col(COL_LY),
                  pl.BlockSpec((LRU_CONV, bd), lambda g: (0, g)),
                  pl.BlockSpec((1, bd), lambda g: (0, g)),
                  gate_w(0), gate_w(0), gate_w(1), gate_w(1),
                  vec2, vec2, vec2],
        out_specs=pl.BlockSpec((s, bd), lambda g: (0, g)),
        out_shape=jax.ShapeDtypeStruct((s, LRU_WIDTH), BF16),
        scratch_shapes=[pltpu.VMEM((s + 2 * LRU_HALO, bd), F32),
                        pltpu.VMEM((s, bd), F32),
                        pltpu.VMEM((LRU_TILE, bd), F32),
                        pltpu.VMEM((LRU_TILE, bd), F32),
                        pltpu.VMEM((LRU_TILE, bd), F32)],
        compiler_params=_params(("parallel",)),
        name="rglru",
    )(proj, proj, w_conv, b_conv, wa, wi, wa, wi, ba, bi, lam)


def _merge_kernel(r_ref, n_ref, l_ref, wr_ref, wn_ref, wl_ref, gr_ref, gn_ref, gl_ref,
                  br_ref, bn_ref, bl_ref, o_ref):
    acc = None
    for x_ref, w_ref, g_ref, b_ref in ((r_ref, wr_ref, gr_ref, br_ref),
                                       (n_ref, wn_ref, gn_ref, bn_ref),
                                       (l_ref, wl_ref, gl_ref, bl_ref)):
        up = jnp.dot(x_ref[...], w_ref[...], preferred_element_type=F32)
        term = jax.nn.sigmoid(g_ref[...] + b_ref[...]) * up
        acc = term if acc is None else acc + term
    o_ref[...] = acc.astype(BF16)


def _merge(ret16, na16, lru16, proj, wb16, gate_b3, layer, bm=512, bn=1024):
    s = ret16.shape[0]
    d = D_MODEL
    kw = BRANCH_WIDTH
    xin = pl.BlockSpec((bm, kw), lambda j, i: (i, 0))
    wsp = lambda b: pl.BlockSpec((None, None, kw, bn), lambda j, i: (layer, b, 0, j))
    gsp = lambda b: pl.BlockSpec((bm, bn), lambda j, i: (i, (COL_GATE + b * d) // bn + j))
    bsp = lambda b: pl.BlockSpec((None, 1, bn), lambda j, i: (layer, 0, b * d // bn + j))
    return pl.pallas_call(
        _merge_kernel,
        grid=(d // bn, s // bm),
        in_specs=[xin, xin, xin, wsp(0), wsp(1), wsp(2), gsp(0), gsp(1), gsp(2), bsp(0), bsp(1), bsp(2)],
        out_specs=pl.BlockSpec((bm, bn), lambda j, i: (i, j)),
        out_shape=jax.ShapeDtypeStruct((s, d), BF16),
        compiler_params=_params(("parallel", "parallel")),
        name="merge",
    )(ret16, na16, lru16, wb16, wb16, wb16, proj, proj, proj, gate_b3, gate_b3, gate_b3)


def _mm_res_ln_kernel(x_ref, w_ref, h_ref, g_ref, b_ref, o32_ref, o16_ref, acc_ref):
    k = pl.program_id(1)

    @pl.when(k == 0)
    def _():
        acc_ref[...] = jnp.zeros_like(acc_ref)

    acc_ref[...] += jnp.dot(x_ref[...], w_ref[...], preferred_element_type=F32)

    @pl.when(k == pl.num_programs(1) - 1)
    def _():
        y = _layer_norm_rows(DEEPNORM_ALPHA * h_ref[...] + acc_ref[...], g_ref[...], b_ref[...])
        o32_ref[...] = y
        o16_ref[...] = y.astype(BF16)


def _mm_res_ln(x16, w16, layer, h32, g, b, name, bm=512, bk=512):
    s, kdim = x16.shape
    d = D_MODEL
    row = pl.BlockSpec((bm, d), lambda i, k: (i, 0))
    vec = pl.BlockSpec((None, 1, d), lambda i, k: (layer, 0, 0))
    return pl.pallas_call(
        _mm_res_ln_kernel,
        grid=(s // bm, kdim // bk),
        in_specs=[pl.BlockSpec((bm, bk), lambda i, k: (i, k)),
                  pl.BlockSpec((None, bk, d), lambda i, k: (layer, k, 0)),
                  row, vec, vec],
        out_specs=[row, row],
        out_shape=[jax.ShapeDtypeStruct((s, d), F32), jax.ShapeDtypeStruct((s, d), BF16)],
        scratch_shapes=[pltpu.VMEM((bm, d), F32)],
        compiler_params=_params(("parallel", "arbitrary")),
        name=name,
    )(x16, w16, h32, g, b)


def _ffn_in_kernel(x_ref, wg_ref, wv_ref, o_ref):
    x = x_ref[...]
    gate = jnp.dot(x, wg_ref[...], preferred_element_type=F32)
    val = jnp.dot(x, wv_ref[...], preferred_element_type=F32)
    o_ref[...] = (gate * jax.nn.sigmoid(gate) * val).astype(BF16)


def _ffn_in(h16, w16, layer, bm=1024, bn=512):
    s, k = h16.shape
    nb = D_FF // bn
    return pl.pallas_call(
        _ffn_in_kernel,
        grid=(nb, s // bm),
        in_specs=[pl.BlockSpec((bm, k), lambda j, i: (i, 0)),
                  pl.BlockSpec((None, k, bn), lambda j, i: (layer, 0, j)),
                  pl.BlockSpec((None, k, bn), lambda j, i: (layer, 0, nb + j))],
        out_specs=pl.BlockSpec((bm, bn), lambda j, i: (i, j)),
        out_shape=jax.ShapeDtypeStruct((s, D_FF), BF16),
        compiler_params=_params(("parallel", "parallel")),
        name="ffn_in",
    )(h16, w16, w16)


def _rotary_tables(seq):
    half = RET_HEAD_DIM // 2
    inv = ROPE_BASE ** (-jnp.arange(half, dtype=F32) / half)
    ang = jnp.arange(seq, dtype=F32)[:, None] * inv[None, :]
    return jnp.cos(ang), jnp.sin(ang)


def _mixer_branches(proj, cos, sin, ret_decay, w_conv, b_conv, lru_wa, lru_ba, lru_wi, lru_bi,
                    lru_lambda, na_rpb):
    q16, k16 = _rope(proj, cos, sin)
    decay_lanes = jnp.broadcast_to(ret_decay.reshape(2 * RET_HEADS, 1, 1),
                                   (2 * RET_HEADS, 1, RET_HEAD_DIM))
    ret16 = _retention(q16, k16, proj, decay_lanes)
    na16 = _neighborhood_attention(proj, na_rpb.reshape(-1))
    lru16 = _rglru(proj, w_conv, b_conv.reshape(1, LRU_WIDTH), lru_wa, lru_ba, lru_wi, lru_bi,
                   lru_lambda)
    return ret16, na16, lru16


def kernel(x, ln_in_g, ln_in_b, w_in, gate_b, ret_decay, w_conv, b_conv, lru_wa, lru_ba, lru_wi, lru_bi, lru_lambda, na_rpb, w_branch, w_out, ln1_g, ln1_b, w_ffn_in, w_ffn_out, ln2_g, ln2_b):
    batch, seq, d = x.shape
    assert batch == 1 and d == D_MODEL
    w_in16 = w_in.astype(BF16)
    w_branch16 = w_branch.astype(BF16)
    w_out16 = w_out.astype(BF16)
    w_ffn_in16 = w_ffn_in.astype(BF16)
    w_ffn_out16 = w_ffn_out.astype(BF16)
    gate_b3 = gate_b.reshape(DEPTH, 1, N_BRANCH * d)
    vec3 = lambda v: v.reshape(DEPTH, 1, d)
    ln1_g3, ln1_b3, ln2_g3, ln2_b3 = vec3(ln1_g), vec3(ln1_b), vec3(ln2_g), vec3(ln2_b)
    cos, sin = _rotary_tables(seq)

    h32, h16 = _ln(x.reshape(seq, d), ln_in_g, ln_in_b)
    for l in range(DEPTH):
        proj = _matmul(h16, w_in16, l, F32)
        ret16, na16, lru16 = _mixer_branches(proj, cos, sin, ret_decay[l], w_conv[l], b_conv[l],
                                             lru_wa[l], lru_ba[l], lru_wi[l], lru_bi[l],
                                             lru_lambda[l], na_rpb[l])
        merged16 = _merge(ret16, na16, lru16, proj, w_branch16, gate_b3, l)
        h32, h16 = _mm_res_ln(merged16, w_out16, l, h32, ln1_g3, ln1_b3, "out_proj", bk=D_MODEL)
        act16 = _ffn_in(h16, w_ffn_in16, l)
        h32, h16 = _mm_res_ln(act16, w_ffn_out16, l, h32, ln2_g3, ln2_b3, "ffn_out", bk=512)
    return h32.reshape(batch, seq, d)
```

```python
import functools
import math

import jax
import jax.numpy as jnp
from jax import lax
from jax.experimental import pallas as pl
from jax.experimental.pallas import tpu as pltpu

F32 = jnp.float32
BF16 = jnp.bfloat16

D_MODEL = 2048
DEPTH = 4
GRID_W = 64
RET_HEADS = 4
RET_HEAD_DIM = 256
NA_HEADS = 8
NA_HEAD_DIM = 128
NA_KH = 8
NA_KW = 16
LRU_WIDTH = 1024
LRU_BLOCKS = 8
LRU_BLOCK_DIM = 128
LRU_CONV = 4
LRU_C = 8.0
N_BRANCH = 3
BRANCH_WIDTH = 1024
IN_COLS = 15360
D_FF = 5632
DEEPNORM_ALPHA = (2 * DEPTH) ** 0.25
LN_EPS = 1e-5
ROPE_BASE = 10000.0

COL_RQ, COL_RK, COL_RV, COL_RG = 0, 1024, 2048, 3072
COL_NQ, COL_NK, COL_NV = 4096, 5120, 6144
COL_LX, COL_LY, COL_GATE = 7168, 8192, 9216

RET_BLOCK = 256

NA_ROWS_PER_STEP = 8
NA_KEY_ROWS = 16
NA_UNIT_ROWS = 2
NA_UNIT_PAIRS = 5
NA_NEG = -1e30

LRU_TILE = 1024
LRU_HALO = 8

VMEM_LIMIT = 56 * 1024 * 1024


def _params(sem, vmem=VMEM_LIMIT):
    return pltpu.CompilerParams(dimension_semantics=sem, vmem_limit_bytes=vmem)


def _softplus(x):
    return jnp.maximum(x, 0.0) + jnp.log1p(jnp.exp(-jnp.abs(x)))


def _log_sigmoid(x):
    return -_softplus(-x)


def _sigmoid(x):
    return 0.5 * jnp.tanh(0.5 * x) + 0.5


def _sqrt_nonneg(x):
    return jnp.where(x > 0.0, x * lax.rsqrt(x), 0.0)


def _layer_norm_rows(x, g, b):
    mu = jnp.mean(x, axis=-1, keepdims=True)
    xc = x - mu
    var = jnp.mean(xc * xc, axis=-1, keepdims=True)
    return xc * lax.rsqrt(var + LN_EPS) * g + b


def _dot(a, b):
    return jnp.dot(a, b, preferred_element_type=F32)


def _dot_nt(a, b):
    return lax.dot_general(a, b, (((1,), (1,)), ((), ())), preferred_element_type=F32)


def _dot_tn(a, b):
    return lax.dot_general(a, b, (((0,), (0,)), ((), ())), preferred_element_type=F32)


def _ln_kernel(x_ref, g_ref, b_ref, o32_ref, o16_ref):
    y = _layer_norm_rows(x_ref[...], g_ref[...], b_ref[...])
    o32_ref[...] = y
    o16_ref[...] = y.astype(BF16)


def _ln(x, g, b, bm=512):
    s, d = x.shape
    row = pl.BlockSpec((bm, d), lambda i: (i, 0))
    vec = pl.BlockSpec((1, d), lambda i: (0, 0))
    return pl.pallas_call(
        _ln_kernel,
        grid=(s // bm,),
        in_specs=[row, vec, vec],
        out_specs=[row, row],
        out_shape=[jax.ShapeDtypeStruct((s, d), F32), jax.ShapeDtypeStruct((s, d), BF16)],
        compiler_params=_params(("parallel",)),
        name="ln_in",
    )(x, g.reshape(1, d), b.reshape(1, d))


def _mm_kernel(x_ref, w_ref, o_ref, w16_ref):
    @pl.when(pl.program_id(1) == 0)
    def _():
        w16_ref[...] = w_ref[...].astype(BF16)

    o_ref[...] = _dot(x_ref[...], w16_ref[...]).astype(o_ref.dtype)


def _matmul(x16, w, layer, out_dtype, bm=1024, bn=1024):
    s, k = x16.shape
    n = w.shape[2]
    return pl.pallas_call(
        _mm_kernel,
        grid=(n // bn, s // bm),
        in_specs=[pl.BlockSpec((bm, k), lambda j, i: (i, 0)),
                  pl.BlockSpec((None, k, bn), lambda j, i: (layer, 0, j))],
        out_specs=pl.BlockSpec((bm, bn), lambda j, i: (i, j)),
        out_shape=jax.ShapeDtypeStruct((s, n), out_dtype),
        scratch_shapes=[pltpu.VMEM((k, bn), BF16)],
        compiler_params=_params(("parallel", "arbitrary")),
        name="in_proj",
    )(x16, w)


def _rope_kernel(q_ref, k_ref, cos_ref, sin_ref, qo_ref, ko_ref):
    c = cos_ref[...]
    s = sin_ref[...]
    half = RET_HEAD_DIM // 2
    k_scale = RET_HEAD_DIM ** -0.5
    for h in range(RET_HEADS):
        lo = slice(h * RET_HEAD_DIM, h * RET_HEAD_DIM + half)
        hi = slice(h * RET_HEAD_DIM + half, (h + 1) * RET_HEAD_DIM)
        q1, q2 = q_ref[:, lo], q_ref[:, hi]
        qo_ref[:, lo] = (q1 * c - q2 * s).astype(BF16)
        qo_ref[:, hi] = (q1 * s + q2 * c).astype(BF16)
        k1, k2 = k_ref[:, lo], k_ref[:, hi]
        ko_ref[:, lo] = ((k1 * c - k2 * s) * k_scale).astype(BF16)
        ko_ref[:, hi] = ((k1 * s + k2 * c) * k_scale).astype(BF16)


def _rope(proj, cos, sin, bm=512):
    s = proj.shape[0]
    w = RET_HEADS * RET_HEAD_DIM
    half = RET_HEAD_DIM // 2
    tab = pl.BlockSpec((bm, half), lambda i: (i, 0))
    out = pl.BlockSpec((bm, w), lambda i: (i, 0))
    return pl.pallas_call(
        _rope_kernel,
        grid=(s // bm,),
        in_specs=[pl.BlockSpec((bm, w), lambda i: (i, COL_RQ // w)),
                  pl.BlockSpec((bm, w), lambda i: (i, COL_RK // w)),
                  tab, tab],
        out_specs=[out, out],
        out_shape=[jax.ShapeDtypeStruct((s, w), BF16)] * 2,
        compiler_params=_params(("parallel",)),
        name="rope",
    )(proj, proj, cos, sin)


def _chunk_pos():
    return lax.broadcasted_iota(jnp.int32, (RET_BLOCK, RET_HEAD_DIM), 0).astype(F32)


def _head_cols(h):
    return slice(h * RET_HEAD_DIM, (h + 1) * RET_HEAD_DIM)


def _ret_bwd_kernel(q_ref, k_ref, v_ref, dl_ref, o_ref, s_ref, qd_ref, kd_ref):
    c = RET_BLOCK

    @pl.when(pl.program_id(0) == 0)
    def _():
        s_ref[...] = jnp.zeros_like(s_ref)
        pos = _chunk_pos()
        for h in range(RET_HEADS):
            lg = _log_sigmoid(dl_ref[RET_HEADS + h])
            qd_ref[h] = jnp.exp((c - pos) * lg)
            kd_ref[h] = jnp.exp(pos * lg)

    for h in range(RET_HEADS):
        cols = _head_cols(h)
        c_dec = jnp.exp(c * _log_sigmoid(dl_ref[RET_HEADS + h]))
        state = s_ref[h]
        o_ref[:, cols] = _dot(q_ref[:, cols], state.astype(BF16)) * qd_ref[h]
        vk = (v_ref[:, cols] * kd_ref[h]).astype(BF16)
        s_ref[h] = c_dec * state + _dot_tn(k_ref[:, cols], vk)


def _ret_fwd_kernel(q_ref, k_ref, v_ref, g_ref, ib_ref, dl_ref, o_ref, s_ref, qd_ref, kd_ref, m_ref):
    c = RET_BLOCK
    assert c <= RET_HEAD_DIM

    @pl.when(pl.program_id(0) == 0)
    def _():
        s_ref[...] = jnp.zeros_like(s_ref)
        pos = _chunk_pos()
        ri = lax.broadcasted_iota(jnp.int32, (c, c), 0)
        ci = lax.broadcasted_iota(jnp.int32, (c, c), 1)
        diff = (ri - ci).astype(F32)
        for h in range(RET_HEADS):
            lgf = _log_sigmoid(dl_ref[h])
            lgb = _log_sigmoid(dl_ref[RET_HEADS + h])
            qd_ref[h] = jnp.exp((pos + 1.0) * lgf)
            kd_ref[h] = jnp.exp((c - 1.0 - pos) * lgf)
            m_ref[h] = jnp.exp(jnp.abs(diff) * jnp.where(diff >= 0, lgf[:, :c], lgb[:, :c]))

    for h in range(RET_HEADS):
        cols = _head_cols(h)
        q = q_ref[:, cols]
        k = k_ref[:, cols]
        v = v_ref[:, cols]
        p = (_dot_nt(q, k) * m_ref[h]).astype(BF16)
        intra = _dot(p, v.astype(BF16))
        state = s_ref[h]
        inter = _dot(q, state.astype(BF16)) * qd_ref[h]
        y = intra + inter + ib_ref[:, cols]

        c_dec = jnp.exp(c * _log_sigmoid(dl_ref[h]))
        vk = (v * kd_ref[h]).astype(BF16)
        s_ref[h] = c_dec * state + _dot_tn(k, vk)

        mu = jnp.mean(y, axis=-1, keepdims=True)
        yc = y - mu
        var = jnp.mean(yc * yc, axis=-1, keepdims=True)
        yn = yc * lax.rsqrt(var + LN_EPS)
        g = g_ref[:, cols]
        o_ref[:, cols] = (g * _sigmoid(g) * yn).astype(BF16)


def _retention(q16, k16, proj, decay_lanes):
    s = q16.shape[0]
    c, d, nh = RET_BLOCK, RET_HEAD_DIM, RET_HEADS
    nc = s // c
    w = nh * d
    dl_spec = pl.BlockSpec((2 * nh, 1, d), lambda n: (0, 0, 0))
    table = pltpu.VMEM((nh, c, d), F32)

    def rev(col):
        return pl.BlockSpec((c, w), lambda n: (nc - 1 - n, col))

    inter_b = pl.pallas_call(
        _ret_bwd_kernel,
        grid=(nc,),
        in_specs=[rev(0), rev(0), rev(COL_RV // w), dl_spec],
        out_specs=rev(0),
        out_shape=jax.ShapeDtypeStruct((s, w), F32),
        scratch_shapes=[pltpu.VMEM((nh, d, d), F32), table, table],
        compiler_params=_params(("arbitrary",)),
        name="ret_bwd",
    )(q16, k16, proj, decay_lanes)

    def fwd(col):
        return pl.BlockSpec((c, w), lambda n: (n, col))

    return pl.pallas_call(
        _ret_fwd_kernel,
        grid=(nc,),
        in_specs=[fwd(0), fwd(0), fwd(COL_RV // w), fwd(COL_RG // w), fwd(0), dl_spec],
        out_specs=fwd(0),
        out_shape=jax.ShapeDtypeStruct((s, w), BF16),
        scratch_shapes=[pltpu.VMEM((nh, d, d), F32), table, table, pltpu.VMEM((nh, c, c), F32)],
        compiler_params=_params(("arbitrary",)),
        name="ret_fwd",
    )(q16, k16, proj, proj, inter_b, decay_lanes)


_NA_SPAN = NA_KEY_ROWS - NA_KH
_NA_START_PAIR = ((0, 0, 0, 1), (0, 1, 2, 3), (2, 3, 3, 3))


def _na_first_key_row(case, i):
    return (max(i - NA_KH // 2, 0), i, min(i + NA_KH // 2, _NA_SPAN))[case]


def _na_build_bias(rpb_ref, bias_ref, head):
    w = GRID_W
    n_dc = 2 * NA_KW - 1
    n_dr = 2 * NA_KH - 1
    lane = lax.broadcasted_iota(jnp.int32, (w, 2 * w), 1)
    cq = lax.broadcasted_iota(jnp.int32, (w, 2 * w), 0)
    ck = lane & (w - 1)
    dc = jnp.clip(ck - cq, -(NA_KW - 1), NA_KW - 1) + (NA_KW - 1)
    col_start = jnp.clip(cq - NA_KW // 2, 0, w - NA_KW)
    in_win = (ck >= col_start) & (ck < col_start + NA_KW)
    dc_is = [dc == e for e in range(n_dc)]
    neg = jnp.full((w, 2 * w), NA_NEG, F32)
    rows = []
    for dr in range(n_dr):
        t = jnp.zeros((w, 2 * w), F32)
        base = (head * n_dr + dr) * n_dc
        for e in range(n_dc):
            t = jnp.where(dc_is[e], rpb_ref[base + e], t)
        rows.append(jnp.where(in_win, t, neg))
    left = lane < w
    for case, off in enumerate((0, _NA_SPAN // 2, _NA_SPAN)):
        for u in range(NA_ROWS_PER_STEP // NA_UNIT_ROWS):
            sp = _NA_START_PAIR[case][u]
            for ii in range(NA_UNIT_ROWS):
                i = u * NA_UNIT_ROWS + ii
                first = _na_first_key_row(case, i)
                assert 2 * sp <= first and first + NA_KH <= 2 * (sp + NA_UNIT_PAIRS)

                def tile(j):
                    if first <= j < first + NA_KH:
                        return rows[j - i + (NA_KH - 1) - off]
                    return None

                for jj in range(NA_UNIT_PAIRS):
                    a, b = tile(2 * (sp + jj)), tile(2 * (sp + jj) + 1)
                    if a is None and b is None:
                        blk = neg
                    else:
                        blk = jnp.where(left, neg if a is None else a, neg if b is None else b)
                    bias_ref[case, u, ii * w:(ii + 1) * w, jj * 2 * w:(jj + 1) * 2 * w] = blk


def _na_kernel(rpb_ref, q_ref, k_ref, v_ref, o_ref, kb_ref, vb_ref, bias_ref, *, rows):
    head = pl.program_id(0)
    rb = pl.program_id(1)
    nrb = rows // NA_ROWS_PER_STEP
    uq = NA_UNIT_ROWS * GRID_W
    uk = NA_UNIT_PAIRS * 2 * GRID_W

    @pl.when(rb == 0)
    def _():
        kb_ref[...] = k_ref[...].astype(BF16)
        vb_ref[...] = v_ref[...].astype(BF16)
        _na_build_bias(rpb_ref, bias_ref, head)

    key_row0 = jnp.clip(rb * NA_ROWS_PER_STEP - NA_KH // 2, 0, rows - NA_KEY_ROWS)
    case = jnp.where(rb == 0, 0, jnp.where(rb == nrb - 1, 2, 1))
    for u in range(NA_ROWS_PER_STEP // NA_UNIT_ROWS):
        first, mid, last = (_NA_START_PAIR[cs][u] for cs in range(3))
        sp = jnp.where(rb == 0, first, jnp.where(rb == nrb - 1, last, mid))
        kstart = pl.multiple_of((key_row0 + 2 * sp) * GRID_W, 2 * GRID_W)
        kk = kb_ref[pl.ds(kstart, uk), :]
        vv = vb_ref[pl.ds(kstart, uk), :]
        q = q_ref[u * uq:(u + 1) * uq, :].astype(BF16)
        s = _dot_nt(q, kk) * (NA_HEAD_DIM ** -0.5) + bias_ref[case, u]
        m = jnp.max(s, axis=-1, keepdims=True)
        e = jnp.exp(s - m)
        l = jnp.sum(e, axis=-1, keepdims=True)
        o = _dot(e.astype(BF16), vv) / l
        o_ref[u * uq:(u + 1) * uq, :] = o.astype(BF16)


def _neighborhood_attention(proj, rpb_flat):
    s = proj.shape[0]
    rows = s // GRID_W
    d = NA_HEAD_DIM
    bq = NA_ROWS_PER_STEP * GRID_W
    nrb = rows // NA_ROWS_PER_STEP
    n_units = NA_ROWS_PER_STEP // NA_UNIT_ROWS
    return pl.pallas_call(
        functools.partial(_na_kernel, rows=rows),
        grid=(NA_HEADS, nrb),
        in_specs=[pl.BlockSpec(memory_space=pltpu.SMEM),
                  pl.BlockSpec((bq, d), lambda h, r: (r, COL_NQ // d + h)),
                  pl.BlockSpec((s, d), lambda h, r: (0, COL_NK // d + h)),
                  pl.BlockSpec((s, d), lambda h, r: (0, COL_NV // d + h))],
        out_specs=pl.BlockSpec((bq, d), lambda h, r: (r, h)),
        out_shape=jax.ShapeDtypeStruct((s, NA_HEADS * d), BF16),
        scratch_shapes=[pltpu.VMEM((s, d), BF16), pltpu.VMEM((s, d), BF16),
                        pltpu.VMEM((3, n_units, NA_UNIT_ROWS * GRID_W, NA_UNIT_PAIRS * 2 * GRID_W), F32)],
        compiler_params=_params(("parallel", "arbitrary")),
        name="natten",
    )(rpb_flat, proj, proj, proj)


def _gelu_tanh(x):
    return x * (0.5 * (1.0 + jnp.tanh(math.sqrt(2.0 / math.pi) * (x + 0.044715 * (x * x * x)))))


def _lru_kernel(x_ref, y_ref, wc_ref, bc_ref, waf_ref, wif_ref, wab_ref, wib_ref, ba_ref, bi_ref,
                lam_ref, o_ref, xpad_ref, hf_ref, a_ref, b_ref, hb_ref, *, seq):
    tt = min(LRU_TILE, seq)
    nt = seq // tt
    ng = tt // 8
    lanes = LRU_BLOCK_DIM
    row = lax.broadcasted_iota(jnp.int32, (8, lanes), 0)

    zeros_halo = jnp.zeros((LRU_HALO, lanes), F32)
    xpad_ref[0:LRU_HALO, :] = zeros_halo
    xpad_ref[LRU_HALO + seq:2 * LRU_HALO + seq, :] = zeros_halo

    def stage(t, carry):
        t0 = pl.multiple_of(t * tt, tt)
        xpad_ref[pl.ds(LRU_HALO + t0, tt), :] = x_ref[pl.ds(t0, tt), :]
        return carry

    lax.fori_loop(0, nt, stage, 0)

    def tile_gates(t0, wa_ref, wi_ref, d):
        base = LRU_HALO - LRU_CONV // 2
        xc = bc_ref[...]
        for j in range(LRU_CONV):
            xc = xc + xpad_ref[pl.ds(t0 + base + j, tt), :] * wc_ref[j:j + 1, :]
        xb = xc.astype(BF16)
        r = _sigmoid(_dot(xb, wa_ref[...].astype(BF16)) + ba_ref[d:d + 1, :])
        i = _sigmoid(_dot(xb, wi_ref[...].astype(BF16)) + bi_ref[d:d + 1, :])
        log_a = (-LRU_C) * r * _softplus(-lam_ref[d:d + 1, :])
        a = jnp.exp(log_a)
        a_ref[...] = a
        b_ref[...] = _sqrt_nonneg(-jnp.tanh(log_a) * (a * a + 1.0)) * (i * xc)

    def fwd_tile(t, carry):
        t0 = pl.multiple_of(t * tt, tt)
        tile_gates(t0, waf_ref, wif_ref, 0)

        def group(gi, c):
            r0 = pl.multiple_of(gi * 8, 8)
            a = a_ref[pl.ds(r0, 8), :]
            b = b_ref[pl.ds(r0, 8), :]
            for sh in (1, 2, 4):
                keep = row >= sh
                a_sh = jnp.where(keep, pltpu.roll(a, sh, 0), 1.0)
                b_sh = jnp.where(keep, pltpu.roll(b, sh, 0), 0.0)
                b = a * b_sh + b
                a = a * a_sh
            h = a * c + b
            hf_ref[pl.ds(t0 + r0, 8), :] = h
            return jnp.broadcast_to(h[7:8, :], (8, lanes))

        return lax.fori_loop(0, ng, group, carry, unroll=8)

    lax.fori_loop(0, nt, fwd_tile, jnp.zeros((8, lanes), F32))

    def bwd_tile(ti, carry):
        t0 = pl.multiple_of((nt - 1 - ti) * tt, tt)
        tile_gates(t0, wab_ref, wib_ref, 1)

        def group(gj, c):
            r0 = pl.multiple_of((ng - 1 - gj) * 8, 8)
            a = a_ref[pl.ds(r0, 8), :]
            b = b_ref[pl.ds(r0, 8), :]
            for sh in (1, 2, 4):
                keep = row < 8 - sh
                a_sh = jnp.where(keep, pltpu.roll(a, 8 - sh, 0), 1.0)
                b_sh = jnp.where(keep, pltpu.roll(b, 8 - sh, 0), 0.0)
                b = a * b_sh + b
                a = a * a_sh
            h = a * c + b
            hb_ref[pl.ds(r0, 8), :] = h
            return jnp.broadcast_to(h[0:1, :], (8, lanes))

        carry = lax.fori_loop(0, ng, group, carry, unroll=8)
        h = hf_ref[pl.ds(t0, tt), :] + hb_ref[...]
        o_ref[pl.ds(t0, tt), :] = (h * _gelu_tanh(y_ref[pl.ds(t0, tt), :])).astype(BF16)
        return carry

    lax.fori_loop(0, nt, bwd_tile, jnp.zeros((8, lanes), F32))


def _rglru(proj, w_conv, b_conv, wa, ba, wi, bi, lam):
    s = proj.shape[0]
    bd = LRU_BLOCK_DIM
    tt = min(LRU_TILE, s)
    col = lambda off: pl.BlockSpec((s, bd), lambda g: (0, off // bd + g))
    gate_w = lambda d: pl.BlockSpec((None, None, bd, bd), lambda g: (d, g, 0, 0))
    vec2 = pl.BlockSpec((2, bd), lambda g: (0, g))
    return pl.pallas_call(
        functools.partial(_lru_kernel, seq=s),
        grid=(LRU_BLOCKS,),
        in_specs=[col(COL_LX), col(COL_LY),
                  pl.BlockSpec((LRU_CONV, bd), lambda g: (0, g)),
                  pl.BlockSpec((1, bd), lambda g: (0, g)),
                  gate_w(0), gate_w(0), gate_w(1), gate_w(1),
                  vec2, vec2, vec2],
        out_specs=pl.BlockSpec((s, bd), lambda g: (0, g)),
        out_shape=jax.ShapeDtypeStruct((s, LRU_WIDTH), BF16),
        scratch_shapes=[pltpu.VMEM((s + 2 * LRU_HALO, bd), F32),
                        pltpu.VMEM((s, bd), F32),
                        pltpu.VMEM((tt, bd), F32),
                        pltpu.VMEM((tt, bd), F32),
                        pltpu.VMEM((tt, bd), F32)],
        compiler_params=_params(("parallel",)),
        name="rglru",
    )(proj, proj, w_conv, b_conv, wa, wi, wa, wi, ba, bi, lam)


def _merge_kernel(r_ref, n_ref, l_ref, wr_ref, wn_ref, wl_ref, gr_ref, gn_ref, gl_ref,
                  br_ref, bn_ref, bl_ref, o_ref):
    acc = None
    for x_ref, w_ref, g_ref, b_ref in ((r_ref, wr_ref, gr_ref, br_ref),
                                       (n_ref, wn_ref, gn_ref, bn_ref),
                                       (l_ref, wl_ref, gl_ref, bl_ref)):
        up = _dot(x_ref[...], w_ref[...])
        term = _sigmoid(g_ref[...] + b_ref[...]) * up
        acc = term if acc is None else acc + term
    o_ref[...] = acc.astype(BF16)


def _merge(ret16, na16, lru16, proj, wb16, gate_b3, layer, bm=512, bn=1024):
    s = ret16.shape[0]
    d = D_MODEL
    kw = BRANCH_WIDTH
    xin = pl.BlockSpec((bm, kw), lambda j, i: (i, 0))
    wsp = lambda b: pl.BlockSpec((None, None, kw, bn), lambda j, i: (layer, b, 0, j))
    gsp = lambda b: pl.BlockSpec((bm, bn), lambda j, i: (i, (COL_GATE + b * d) // bn + j))
    bsp = lambda b: pl.BlockSpec((None, 1, bn), lambda j, i: (layer, 0, b * d // bn + j))
    return pl.pallas_call(
        _merge_kernel,
        grid=(d // bn, s // bm),
        in_specs=[xin, xin, xin, wsp(0), wsp(1), wsp(2), gsp(0), gsp(1), gsp(2), bsp(0), bsp(1), bsp(2)],
        out_specs=pl.BlockSpec((bm, bn), lambda j, i: (i, j)),
        out_shape=jax.ShapeDtypeStruct((s, d), BF16),
        compiler_params=_params(("parallel", "parallel")),
        name="merge",
    )(ret16, na16, lru16, wb16, wb16, wb16, proj, proj, proj, gate_b3, gate_b3, gate_b3)


def _mm_res_ln_kernel(x_ref, w_ref, h_ref, g_ref, b_ref, o32_ref, o16_ref):
    k = pl.program_id(1)

    @pl.when(k == 0)
    def _():
        o32_ref[...] = DEEPNORM_ALPHA * h_ref[...]

    o32_ref[...] += _dot(x_ref[...], w_ref[...])

    @pl.when(k == pl.num_programs(1) - 1)
    def _():
        y = _layer_norm_rows(o32_ref[...], g_ref[...], b_ref[...])
        o32_ref[...] = y
        o16_ref[...] = y.astype(BF16)


def _mm_res_ln(x16, w16, layer, h32, g, b, name, bm, bk):
    s, kdim = x16.shape
    d = D_MODEL
    row = pl.BlockSpec((bm, d), lambda i, k: (i, 0))
    vec = pl.BlockSpec((None, 1, d), lambda i, k: (layer, 0, 0))
    return pl.pallas_call(
        _mm_res_ln_kernel,
        grid=(s // bm, kdim // bk),
        in_specs=[pl.BlockSpec((bm, bk), lambda i, k: (i, k)),
                  pl.BlockSpec((None, bk, d), lambda i, k: (layer, k, 0)),
                  row, vec, vec],
        out_specs=[row, row],
        out_shape=[jax.ShapeDtypeStruct((s, d), F32), jax.ShapeDtypeStruct((s, d), BF16)],
        compiler_params=_params(("parallel", "arbitrary")),
        name=name,
    )(x16, w16, h32, g, b)


def _ffn_in_kernel(x_ref, wg_ref, wv_ref, o_ref, wg16_ref, wv16_ref):
    @pl.when(pl.program_id(1) == 0)
    def _():
        wg16_ref[...] = wg_ref[...].astype(BF16)
        wv16_ref[...] = wv_ref[...].astype(BF16)

    x = x_ref[...]
    gate = _dot(x, wg16_ref[...])
    val = _dot(x, wv16_ref[...])
    o_ref[...] = (gate * _sigmoid(gate) * val).astype(BF16)


def _ffn_in(h16, w, layer, bm=1024, bn=512):
    s, k = h16.shape
    nb = D_FF // bn
    return pl.pallas_call(
        _ffn_in_kernel,
        grid=(nb, s // bm),
        in_specs=[pl.BlockSpec((bm, k), lambda j, i: (i, 0)),
                  pl.BlockSpec((None, k, bn), lambda j, i: (layer, 0, j)),
                  pl.BlockSpec((None, k, bn), lambda j, i: (layer, 0, nb + j))],
        out_specs=pl.BlockSpec((bm, bn), lambda j, i: (i, j)),
        out_shape=jax.ShapeDtypeStruct((s, D_FF), BF16),
        scratch_shapes=[pltpu.VMEM((k, bn), BF16), pltpu.VMEM((k, bn), BF16)],
        compiler_params=_params(("parallel", "arbitrary")),
        name="ffn_in",
    )(h16, w, w)


def _rotary_tables(seq):
    half = RET_HEAD_DIM // 2
    inv = ROPE_BASE ** (-jnp.arange(half, dtype=F32) / half)
    ang = jnp.arange(seq, dtype=F32)[:, None] * inv[None, :]
    return jnp.cos(ang), jnp.sin(ang)


def _mixer_branches(proj, cos, sin, ret_decay, w_conv, b_conv, lru_wa, lru_ba, lru_wi, lru_bi,
                    lru_lambda, na_rpb):
    q16, k16 = _rope(proj, cos, sin)
    decay_lanes = jnp.broadcast_to(ret_decay.reshape(2 * RET_HEADS, 1, 1),
                                   (2 * RET_HEADS, 1, RET_HEAD_DIM))
    ret16 = _retention(q16, k16, proj, decay_lanes)
    na16 = _neighborhood_attention(proj, na_rpb.reshape(-1))
    lru16 = _rglru(proj, w_conv, b_conv.reshape(1, LRU_WIDTH), lru_wa, lru_ba, lru_wi, lru_bi,
                   lru_lambda)
    return ret16, na16, lru16


def kernel(x, ln_in_g, ln_in_b, w_in, gate_b, ret_decay, w_conv, b_conv, lru_wa, lru_ba, lru_wi, lru_bi, lru_lambda, na_rpb, w_branch, w_out, ln1_g, ln1_b, w_ffn_in, w_ffn_out, ln2_g, ln2_b):
    batch, seq, d = x.shape
    assert batch == 1 and d == D_MODEL
    w_branch16 = w_branch.astype(BF16)
    w_out16 = w_out.astype(BF16)
    w_ffn_out16 = w_ffn_out.astype(BF16)
    gate_b3 = gate_b.reshape(DEPTH, 1, N_BRANCH * d)
    vec3 = lambda v: v.reshape(DEPTH, 1, d)
    ln1_g3, ln1_b3, ln2_g3, ln2_b3 = vec3(ln1_g), vec3(ln1_b), vec3(ln2_g), vec3(ln2_b)
    cos, sin = _rotary_tables(seq)

    h32, h16 = _ln(x.reshape(seq, d), ln_in_g, ln_in_b)
    for l in range(DEPTH):
        proj = _matmul(h16, w_in, l, F32)
        ret16, na16, lru16 = _mixer_branches(proj, cos, sin, ret_decay[l], w_conv[l], b_conv[l],
                                             lru_wa[l], lru_ba[l], lru_wi[l], lru_bi[l],
                                             lru_lambda[l], na_rpb[l])
        merged16 = _merge(ret16, na16, lru16, proj, w_branch16, gate_b3, l)
        h32, h16 = _mm_res_ln(merged16, w_out16, l, h32, ln1_g3, ln1_b3, "out_proj", bm=512, bk=D_MODEL)
        act16 = _ffn_in(h16, w_ffn_in, l)
        h32, h16 = _mm_res_ln(act16, w_ffn_out16, l, h32, ln2_g3, ln2_b3, "ffn_out", bm=1024, bk=512)
    return h32.reshape(batch, seq, d)
```

```python
import functools
import math

import jax
import jax.numpy as jnp
from jax import lax
from jax.experimental import pallas as pl
from jax.experimental.pallas import tpu as pltpu

F32 = jnp.float32
BF16 = jnp.bfloat16

D_MODEL = 2048
DEPTH = 4
GRID_W = 64
RET_HEADS = 4
RET_HEAD_DIM = 256
NA_HEADS = 8
NA_HEAD_DIM = 128
NA_KH = 8
NA_KW = 16
LRU_WIDTH = 1024
LRU_BLOCKS = 8
LRU_BLOCK_DIM = 128
LRU_CONV = 4
LRU_C = 8.0
N_BRANCH = 3
BRANCH_WIDTH = 1024
IN_COLS = 15360
D_FF = 5632
DEEPNORM_ALPHA = (2 * DEPTH) ** 0.25
LN_EPS = 1e-5
ROPE_BASE = 10000.0

COL_MID, COL_GATE = 2048, 9216
MID_RV, MID_RG, MID_NQ, MID_NK, MID_NV, MID_LX, MID_LY = (i * 1024 for i in range(7))
IN_PROJ_ROW_GROUP = 256

RET_BLOCK = 256

NA_ROWS_PER_STEP = 8
NA_KEY_ROWS = 16
NA_UNIT_ROWS = 4
NA_UNIT_PAIRS = 6
NA_HEADS_PER_STEP = 2
NA_NEG = -1e30

LRU_TILE = 1024
LRU_HALO = 8

LN_ROW_GROUPS = 4

VMEM_LIMIT = 56 * 1024 * 1024


def _params(sem, vmem=VMEM_LIMIT):
    return pltpu.CompilerParams(dimension_semantics=sem, vmem_limit_bytes=vmem)


def _softplus(x):
    return jnp.maximum(x, 0.0) + jnp.log1p(jnp.exp(-jnp.abs(x)))


def _log_sigmoid(x):
    return -_softplus(-x)


def _sigmoid(x):
    return 0.5 * jnp.tanh(0.5 * x) + 0.5


def _sqrt_nonneg(x):
    return jnp.where(x > 0.0, x * lax.rsqrt(x), 0.0)


def _layer_norm_rows(x, g, b):
    mu = jnp.mean(x, axis=-1, keepdims=True)
    xc = x - mu
    var = jnp.mean(xc * xc, axis=-1, keepdims=True)
    return xc * lax.rsqrt(var + LN_EPS) * g + b


def _dot(a, b):
    return jnp.dot(a, b, preferred_element_type=F32)


def _dot_nt(a, b):
    return lax.dot_general(a, b, (((1,), (1,)), ((), ())), preferred_element_type=F32)


def _dot_tn(a, b):
    return lax.dot_general(a, b, (((0,), (0,)), ((), ())), preferred_element_type=F32)


def _ln_kernel(x_ref, g_ref, b_ref, o32_ref, o16_ref):
    y = _layer_norm_rows(x_ref[...], g_ref[...], b_ref[...])
    o32_ref[...] = y
    o16_ref[...] = y.astype(BF16)


def _ln(x, g, b, bm=512):
    s, d = x.shape
    row = pl.BlockSpec((bm, d), lambda i: (i, 0))
    vec = pl.BlockSpec((1, d), lambda i: (0, 0))
    return pl.pallas_call(
        _ln_kernel,
        grid=(s // bm,),
        in_specs=[row, vec, vec],
        out_specs=[row, row],
        out_shape=[jax.ShapeDtypeStruct((s, d), F32), jax.ShapeDtypeStruct((s, d), BF16)],
        compiler_params=_params(("parallel",)),
        name="ln_in",
    )(x, g.reshape(1, d), b.reshape(1, d))


def _in_proj_kernel(*refs, epilogue):
    if epilogue == "rope":
        x_ref, w_ref, cos_ref, sin_ref, o_ref, w16_ref = refs
    elif epilogue == "gate":
        x_ref, w_ref, gb_ref, o_ref, w16_ref = refs
    else:
        x_ref, w_ref, o_ref, w16_ref = refs

    @pl.when(pl.program_id(1) == 0)
    def _():
        w16_ref[...] = w_ref[...].astype(BF16)

    bm = o_ref.shape[0]
    rg = min(bm, IN_PROJ_ROW_GROUP)
    half = RET_HEAD_DIM // 2
    for r in range(bm // rg):
        rows = slice(r * rg, (r + 1) * rg)
        acc = _dot(x_ref[rows, :], w16_ref[...])
        if epilogue == "rope":
            c = cos_ref[rows, :]
            s = sin_ref[rows, :]
            scale = jnp.where(pl.program_id(0) == 1, RET_HEAD_DIM ** -0.5, 1.0)
            for h in range(RET_HEADS):
                lo = slice(h * RET_HEAD_DIM, h * RET_HEAD_DIM + half)
                hi = slice(h * RET_HEAD_DIM + half, (h + 1) * RET_HEAD_DIM)
                t1, t2 = acc[:, lo], acc[:, hi]
                o_ref[rows, lo] = ((t1 * c - t2 * s) * scale).astype(BF16)
                o_ref[rows, hi] = ((t1 * s + t2 * c) * scale).astype(BF16)
        elif epilogue == "gate":
            o_ref[rows, :] = _sigmoid(acc + gb_ref[...]).astype(BF16)
        else:
            o_ref[rows, :] = acc.astype(BF16)


def _in_proj(x16, w, layer, col0, ncols, epilogue, extras=(), bm=2048, bn=1024):
    s, k = x16.shape
    bm = min(bm, s)
    in_specs = [pl.BlockSpec((bm, k), lambda j, i: (i, 0)),
                pl.BlockSpec((None, k, bn), lambda j, i: (layer, 0, col0 // bn + j))]
    if epilogue == "rope":
        assert ncols == 2 * RET_HEADS * RET_HEAD_DIM and bn == RET_HEADS * RET_HEAD_DIM
        tab = pl.BlockSpec((bm, RET_HEAD_DIM // 2), lambda j, i: (i, 0))
        in_specs += [tab, tab]
    elif epilogue == "gate":
        in_specs += [pl.BlockSpec((None, 1, bn), lambda j, i: (layer, 0, j))]
    return pl.pallas_call(
        functools.partial(_in_proj_kernel, epilogue=epilogue),
        grid=(ncols // bn, s // bm),
        in_specs=in_specs,
        out_specs=pl.BlockSpec((bm, bn), lambda j, i: (i, j)),
        out_shape=jax.ShapeDtypeStruct((s, ncols), BF16),
        scratch_shapes=[pltpu.VMEM((k, bn), BF16)],
        compiler_params=_params(("parallel", "arbitrary")),
        name="in_proj_" + epilogue,
    )(x16, w, *extras)


def _chunk_pos():
    return lax.broadcasted_iota(jnp.int32, (RET_BLOCK, RET_HEAD_DIM), 0).astype(F32)


def _head_cols(h):
    return slice(h * RET_HEAD_DIM, (h + 1) * RET_HEAD_DIM)


def _ret_bwd_kernel(q_ref, k_ref, v_ref, dl_ref, o_ref, s_ref, qd_ref, kd_ref):
    c = RET_BLOCK

    @pl.when(pl.program_id(0) == 0)
    def _():
        s_ref[...] = jnp.zeros_like(s_ref)
        pos = _chunk_pos()
        for h in range(RET_HEADS):
            lg = _log_sigmoid(dl_ref[RET_HEADS + h])
            qd_ref[h] = jnp.exp((c - pos) * lg)
            kd_ref[h] = jnp.exp(pos * lg)

    for h in range(RET_HEADS):
        cols = _head_cols(h)
        c_dec = jnp.exp(c * _log_sigmoid(dl_ref[RET_HEADS + h]))
        state = s_ref[h]
        o_ref[:, cols] = _dot(q_ref[:, cols], state.astype(BF16)) * qd_ref[h]
        vk = (v_ref[:, cols].astype(F32) * kd_ref[h]).astype(BF16)
        s_ref[h] = c_dec * state + _dot_tn(k_ref[:, cols], vk)


def _ret_fwd_kernel(q_ref, k_ref, v_ref, g_ref, ib_ref, dl_ref, o_ref, s_ref, qd_ref, kd_ref, m_ref):
    c = RET_BLOCK
    assert c <= RET_HEAD_DIM

    @pl.when(pl.program_id(0) == 0)
    def _():
        s_ref[...] = jnp.zeros_like(s_ref)
        pos = _chunk_pos()
        ri = lax.broadcasted_iota(jnp.int32, (c, c), 0)
        ci = lax.broadcasted_iota(jnp.int32, (c, c), 1)
        diff = (ri - ci).astype(F32)
        for h in range(RET_HEADS):
            lgf = _log_sigmoid(dl_ref[h])
            lgb = _log_sigmoid(dl_ref[RET_HEADS + h])
            qd_ref[h] = jnp.exp((pos + 1.0) * lgf)
            kd_ref[h] = jnp.exp((c - 1.0 - pos) * lgf)
            m_ref[h] = jnp.exp(jnp.abs(diff) * jnp.where(diff >= 0, lgf[:, :c], lgb[:, :c]))

    for h in range(RET_HEADS):
        cols = _head_cols(h)
        q = q_ref[:, cols]
        k = k_ref[:, cols]
        v = v_ref[:, cols]
        p = (_dot_nt(q, k) * m_ref[h]).astype(BF16)
        intra = _dot(p, v)
        state = s_ref[h]
        inter = _dot(q, state.astype(BF16)) * qd_ref[h]
        y = intra + inter + ib_ref[:, cols]

        c_dec = jnp.exp(c * _log_sigmoid(dl_ref[h]))
        vk = (v.astype(F32) * kd_ref[h]).astype(BF16)
        s_ref[h] = c_dec * state + _dot_tn(k, vk)

        mu = jnp.mean(y, axis=-1, keepdims=True)
        yc = y - mu
        var = jnp.mean(yc * yc, axis=-1, keepdims=True)
        yn = yc * lax.rsqrt(var + LN_EPS)
        g = g_ref[:, cols].astype(F32)
        o_ref[:, cols] = (g * _sigmoid(g) * yn).astype(BF16)


def _retention(qk16, mid16, decay_lanes):
    s = qk16.shape[0]
    c, d, nh = RET_BLOCK, RET_HEAD_DIM, RET_HEADS
    nc = s // c
    w = nh * d
    dl_spec = pl.BlockSpec((2 * nh, 1, d), lambda n: (0, 0, 0))
    table = pltpu.VMEM((nh, c, d), F32)

    def rev(col):
        return pl.BlockSpec((c, w), lambda n: (nc - 1 - n, col))

    inter_b = pl.pallas_call(
        _ret_bwd_kernel,
        grid=(nc,),
        in_specs=[rev(0), rev(1), rev(MID_RV // w), dl_spec],
        out_specs=rev(0),
        out_shape=jax.ShapeDtypeStruct((s, w), F32),
        scratch_shapes=[pltpu.VMEM((nh, d, d), F32), table, table],
        compiler_params=_params(("arbitrary",)),
        name="ret_bwd",
    )(qk16, qk16, mid16, decay_lanes)

    def fwd(col):
        return pl.BlockSpec((c, w), lambda n: (n, col))

    return pl.pallas_call(
        _ret_fwd_kernel,
        grid=(nc,),
        in_specs=[fwd(0), fwd(1), fwd(MID_RV // w), fwd(MID_RG // w), fwd(0), dl_spec],
        out_specs=fwd(0),
        out_shape=jax.ShapeDtypeStruct((s, w), BF16),
        scratch_shapes=[pltpu.VMEM((nh, d, d), F32), table, table, pltpu.VMEM((nh, c, c), F32)],
        compiler_params=_params(("arbitrary",)),
        name="ret_fwd",
    )(qk16, qk16, mid16, mid16, inter_b, decay_lanes)


_NA_SPAN = NA_KEY_ROWS - NA_KH
_NA_START_PAIR = ((0, 0), (0, 2), (2, 2))


def _na_first_key_row(case, i):
    return (max(i - NA_KH // 2, 0), i, min(i + NA_KH // 2, _NA_SPAN))[case]


def _na_build_bias(rpb_ref, bias_ref, head):
    w = GRID_W
    n_dc = 2 * NA_KW - 1
    n_dr = 2 * NA_KH - 1
    lane = lax.broadcasted_iota(jnp.int32, (w, 2 * w), 1)
    cq = lax.broadcasted_iota(jnp.int32, (w, 2 * w), 0)
    ck = lane & (w - 1)
    dc = jnp.clip(ck - cq, -(NA_KW - 1), NA_KW - 1) + (NA_KW - 1)
    col_start = jnp.clip(cq - NA_KW // 2, 0, w - NA_KW)
    in_win = (ck >= col_start) & (ck < col_start + NA_KW)
    dc_is = [dc == e for e in range(n_dc)]
    neg = jnp.full((w, 2 * w), NA_NEG, F32)
    rows = []
    for dr in range(n_dr):
        t = jnp.zeros((w, 2 * w), F32)
        base = (head * n_dr + dr) * n_dc
        for e in range(n_dc):
            t = jnp.where(dc_is[e], rpb_ref[base + e], t)
        rows.append(jnp.where(in_win, t, neg))
    left = lane < w
    for case, off in enumerate((0, _NA_SPAN // 2, _NA_SPAN)):
        for u in range(NA_ROWS_PER_STEP // NA_UNIT_ROWS):
            sp = _NA_START_PAIR[case][u]
            for ii in range(NA_UNIT_ROWS):
                i = u * NA_UNIT_ROWS + ii
                first = _na_first_key_row(case, i)
                assert 2 * sp <= first and first + NA_KH <= 2 * (sp + NA_UNIT_PAIRS)

                def tile(j):
                    if first <= j < first + NA_KH:
                        return rows[j - i + (NA_KH - 1) - off]
                    return None

                for jj in range(NA_UNIT_PAIRS):
                    a, b = tile(2 * (sp + jj)), tile(2 * (sp + jj) + 1)
                    if a is None and b is None:
                        blk = neg
                    else:
                        blk = jnp.where(left, neg if a is None else a, neg if b is None else b)
                    bias_ref[case, u, ii * w:(ii + 1) * w, jj * 2 * w:(jj + 1) * 2 * w] = blk


def _na_kernel(rpb_ref, q_ref, k_ref, v_ref, o_ref, bias_ref, *, rows):
    rb = pl.program_id(1)
    nrb = rows // NA_ROWS_PER_STEP
    d = NA_HEAD_DIM
    uq = NA_UNIT_ROWS * GRID_W
    uk = NA_UNIT_PAIRS * 2 * GRID_W

    @pl.when(rb == 0)
    def _():
        for hh in range(NA_HEADS_PER_STEP):
            _na_build_bias(rpb_ref, bias_ref.at[hh], pl.program_id(0) * NA_HEADS_PER_STEP + hh)

    key_row0 = jnp.clip(rb * NA_ROWS_PER_STEP - NA_KH // 2, 0, rows - NA_KEY_ROWS)
    case = jnp.where(rb == 0, 0, jnp.where(rb == nrb - 1, 2, 1))
    for u in range(NA_ROWS_PER_STEP // NA_UNIT_ROWS):
        first, mid, last = (_NA_START_PAIR[cs][u] for cs in range(3))
        sp = jnp.where(rb == 0, first, jnp.where(rb == nrb - 1, last, mid))
        kstart = pl.multiple_of((key_row0 + 2 * sp) * GRID_W, 2 * GRID_W)
        for hh in range(NA_HEADS_PER_STEP):
            cols = slice(hh * d, (hh + 1) * d)
            kk = k_ref[pl.ds(kstart, uk), cols]
            vv = v_ref[pl.ds(kstart, uk), cols]
            q = q_ref[u * uq:(u + 1) * uq, cols]
            s = _dot_nt(q, kk) * (NA_HEAD_DIM ** -0.5) + bias_ref[hh, case, u]
            m = jnp.max(s, axis=-1, keepdims=True)
            e = jnp.exp(s - m)
            l = jnp.sum(e, axis=-1, keepdims=True)
            o = _dot(e.astype(BF16), vv) / l
            o_ref[u * uq:(u + 1) * uq, cols] = o.astype(BF16)


def _neighborhood_attention(mid16, rpb_flat):
    s = mid16.shape[0]
    rows = s // GRID_W
    d = NA_HEAD_DIM * NA_HEADS_PER_STEP
    bq = NA_ROWS_PER_STEP * GRID_W
    nrb = rows // NA_ROWS_PER_STEP
    n_units = NA_ROWS_PER_STEP // NA_UNIT_ROWS
    return pl.pallas_call(
        functools.partial(_na_kernel, rows=rows),
        grid=(NA_HEADS // NA_HEADS_PER_STEP, nrb),
        in_specs=[pl.BlockSpec(memory_space=pltpu.SMEM),
                  pl.BlockSpec((bq, d), lambda h, r: (r, MID_NQ // d + h)),
                  pl.BlockSpec((s, d), lambda h, r: (0, MID_NK // d + h)),
                  pl.BlockSpec((s, d), lambda h, r: (0, MID_NV // d + h))],
        out_specs=pl.BlockSpec((bq, d), lambda h, r: (r, h)),
        out_shape=jax.ShapeDtypeStruct((s, NA_HEADS * NA_HEAD_DIM), BF16),
        scratch_shapes=[pltpu.VMEM((NA_HEADS_PER_STEP, 3, n_units, NA_UNIT_ROWS * GRID_W,
                                    NA_UNIT_PAIRS * 2 * GRID_W), F32)],
        compiler_params=_params(("parallel", "arbitrary")),
        name="natten",
    )(rpb_flat, mid16, mid16, mid16)


def _gelu_tanh(x):
    return x * (0.5 * (1.0 + jnp.tanh(math.sqrt(2.0 / math.pi) * (x + 0.044715 * (x * x * x)))))


def _lru_kernel(x_ref, y_ref, wc_ref, bc_ref, waf_ref, wif_ref, wab_ref, wib_ref, ba_ref, bi_ref,
                lam_ref, o_ref, xpad_ref, xc_ref, hf_ref, a_ref, b_ref, hb_ref, *, seq):
    tt = min(LRU_TILE, seq)
    nt = seq // tt
    ng = tt // 8
    lanes = LRU_BLOCK_DIM
    row = lax.broadcasted_iota(jnp.int32, (8, lanes), 0)

    zeros_halo = jnp.zeros((LRU_HALO, lanes), F32)
    xpad_ref[0:LRU_HALO, :] = zeros_halo
    xpad_ref[LRU_HALO + seq:2 * LRU_HALO + seq, :] = zeros_halo

    def stage(t, carry):
        t0 = pl.multiple_of(t * tt, tt)
        xpad_ref[pl.ds(LRU_HALO + t0, tt), :] = x_ref[pl.ds(t0, tt), :].astype(F32)
        return carry

    lax.fori_loop(0, nt, stage, 0)

    def conv_tile(t0):
        base = LRU_HALO - LRU_CONV // 2
        xc = bc_ref[...]
        for j in range(LRU_CONV):
            xc = xc + xpad_ref[pl.ds(t0 + base + j, tt), :] * wc_ref[j:j + 1, :]
        return xc

    def tile_gates(xc, wa_ref, wi_ref, d):
        xb = xc.astype(BF16)
        t_r = jnp.tanh(_dot(xb, (0.5 * wa_ref[...]).astype(BF16)) + 0.5 * ba_ref[d:d + 1, :])
        t_i = jnp.tanh(_dot(xb, (0.5 * wi_ref[...]).astype(BF16)) + 0.5 * bi_ref[d:d + 1, :])
        neg_log_a = (0.5 * LRU_C * _softplus(-lam_ref[d:d + 1, :])) * (t_r + 1.0)
        a = jnp.exp(-neg_log_a)
        a_ref[...] = a
        gain = _sqrt_nonneg(jnp.tanh(neg_log_a) * (a * a + 1.0))
        b_ref[...] = gain * ((0.5 * t_i + 0.5) * xc)

    def fwd_tile(t, carry):
        t0 = pl.multiple_of(t * tt, tt)
        xc = conv_tile(t0)
        xc_ref[pl.ds(t0, tt), :] = xc
        tile_gates(xc, waf_ref, wif_ref, 0)

        def group(gi, c):
            r0 = pl.multiple_of(gi * 8, 8)
            a = a_ref[pl.ds(r0, 8), :]
            b = b_ref[pl.ds(r0, 8), :]
            for sh in (1, 2, 4):
                keep = row >= sh
                a_sh = jnp.where(keep, pltpu.roll(a, sh, 0), 1.0)
                b_sh = jnp.where(keep, pltpu.roll(b, sh, 0), 0.0)
                b = a * b_sh + b
                a = a * a_sh
            h = a * c + b
            hf_ref[pl.ds(t0 + r0, 8), :] = h
            return jnp.broadcast_to(h[7:8, :], (8, lanes))

        return lax.fori_loop(0, ng, group, carry, unroll=8)

    lax.fori_loop(0, nt, fwd_tile, jnp.zeros((8, lanes), F32))

    def bwd_tile(ti, carry):
        t0 = pl.multiple_of((nt - 1 - ti) * tt, tt)
        tile_gates(xc_ref[pl.ds(t0, tt), :], wab_ref, wib_ref, 1)

        def group(gj, c):
            r0 = pl.multiple_of((ng - 1 - gj) * 8, 8)
            a = a_ref[pl.ds(r0, 8), :]
            b = b_ref[pl.ds(r0, 8), :]
            for sh in (1, 2, 4):
                keep = row < 8 - sh
                a_sh = jnp.where(keep, pltpu.roll(a, 8 - sh, 0), 1.0)
                b_sh = jnp.where(keep, pltpu.roll(b, 8 - sh, 0), 0.0)
                b = a * b_sh + b
                a = a * a_sh
            h = a * c + b
            hb_ref[pl.ds(r0, 8), :] = h
            return jnp.broadcast_to(h[0:1, :], (8, lanes))

        carry = lax.fori_loop(0, ng, group, carry, unroll=8)
        h = hf_ref[pl.ds(t0, tt), :] + hb_ref[...]
        y = y_ref[pl.ds(t0, tt), :].astype(F32)
        o_ref[pl.ds(t0, tt), :] = (h * _gelu_tanh(y)).astype(BF16)
        return carry

    lax.fori_loop(0, nt, bwd_tile, jnp.zeros((8, lanes), F32))


def _rglru(mid16, w_conv, b_conv, wa, ba, wi, bi, lam):
    s = mid16.shape[0]
    bd = LRU_BLOCK_DIM
    tt = min(LRU_TILE, s)
    col = lambda off: pl.BlockSpec((s, bd), lambda g: (0, off // bd + g))
    gate_w = lambda d: pl.BlockSpec((None, None, bd, bd), lambda g: (d, g, 0, 0))
    vec2 = pl.BlockSpec((2, bd), lambda g: (0, g))
    return pl.pallas_call(
        functools.partial(_lru_kernel, seq=s),
        grid=(LRU_BLOCKS,),
        in_specs=[col(MID_LX), col(MID_LY),
                  pl.BlockSpec((LRU_CONV, bd), lambda g: (0, g)),
                  pl.BlockSpec((1, bd), lambda g: (0, g)),
                  gate_w(0), gate_w(0), gate_w(1), gate_w(1),
                  vec2, vec2, vec2],
        out_specs=pl.BlockSpec((s, bd), lambda g: (0, g)),
        out_shape=jax.ShapeDtypeStruct((s, LRU_WIDTH), BF16),
        scratch_shapes=[pltpu.VMEM((s + 2 * LRU_HALO, bd), F32),
                        pltpu.VMEM((s, bd), F32),
                        pltpu.VMEM((s, bd), F32),
                        pltpu.VMEM((tt, bd), F32),
                        pltpu.VMEM((tt, bd), F32),
                        pltpu.VMEM((tt, bd), F32)],
        compiler_params=_params(("parallel",)),
        name="rglru",
    )(mid16, mid16, w_conv, b_conv, wa, wi, wa, wi, ba, bi, lam)


def _merge_kernel(r_ref, n_ref, l_ref, wr_ref, wn_ref, wl_ref, gr_ref, gn_ref, gl_ref, o_ref,
                  wr16_ref, wn16_ref, wl16_ref):
    @pl.when(pl.program_id(1) == 0)
    def _():
        for w_ref, w16_ref in ((wr_ref, wr16_ref), (wn_ref, wn16_ref), (wl_ref, wl16_ref)):
            w16_ref[...] = w_ref[...].astype(BF16)

    acc = None
    for x_ref, w16_ref, g_ref in ((r_ref, wr16_ref, gr_ref), (n_ref, wn16_ref, gn_ref),
                                  (l_ref, wl16_ref, gl_ref)):
        term = g_ref[...].astype(F32) * _dot(x_ref[...], w16_ref[...])
        acc = term if acc is None else acc + term
    o_ref[...] = acc.astype(BF16)


def _merge(ret16, na16, lru16, gate16, w_branch, layer, bm=512, bn=1024):
    s = ret16.shape[0]
    d = D_MODEL
    kw = BRANCH_WIDTH
    xin = pl.BlockSpec((bm, kw), lambda j, i: (i, 0))
    wsp = lambda b: pl.BlockSpec((None, None, kw, bn), lambda j, i: (layer, b, 0, j))
    gsp = lambda b: pl.BlockSpec((bm, bn), lambda j, i: (i, b * d // bn + j))
    return pl.pallas_call(
        _merge_kernel,
        grid=(d // bn, s // bm),
        in_specs=[xin, xin, xin, wsp(0), wsp(1), wsp(2), gsp(0), gsp(1), gsp(2)],
        out_specs=pl.BlockSpec((bm, bn), lambda j, i: (i, j)),
        out_shape=jax.ShapeDtypeStruct((s, d), BF16),
        scratch_shapes=[pltpu.VMEM((kw, bn), BF16)] * N_BRANCH,
        compiler_params=_params(("parallel", "arbitrary")),
        name="merge",
    )(ret16, na16, lru16, w_branch, w_branch, w_branch, gate16, gate16, gate16)


def _mm_res_ln_kernel(x_ref, w_ref, h_ref, g_ref, b_ref, o32_ref, o16_ref, *, nk):
    k = pl.program_id(1)
    bm = o32_ref.shape[0]

    def finish(partial):
        rg = bm // LN_ROW_GROUPS
        for rows in (slice(r * rg, (r + 1) * rg) for r in range(LN_ROW_GROUPS)):
            y = _layer_norm_rows(partial(rows) + _dot(x_ref[rows, :], w_ref[...]), g_ref[...], b_ref[...])
            o32_ref[rows, :] = y
            o16_ref[rows, :] = y.astype(BF16)

    if nk == 1:
        finish(lambda rows: DEEPNORM_ALPHA * h_ref[rows, :])
        return

    @pl.when(k == 0)
    def _():
        o32_ref[...] = DEEPNORM_ALPHA * h_ref[...] + _dot(x_ref[...], w_ref[...])

    @pl.when(jnp.logical_and(k > 0, k < nk - 1))
    def _():
        o32_ref[...] += _dot(x_ref[...], w_ref[...])

    @pl.when(k == nk - 1)
    def _():
        finish(lambda rows: o32_ref[rows, :])


def _mm_res_ln(x16, w16, layer, h32, g, b, name, bm, bk):
    s, kdim = x16.shape
    d = D_MODEL
    row = pl.BlockSpec((bm, d), lambda i, k: (i, 0))
    vec = pl.BlockSpec((None, 1, d), lambda i, k: (layer, 0, 0))
    return pl.pallas_call(
        functools.partial(_mm_res_ln_kernel, nk=kdim // bk),
        grid=(s // bm, kdim // bk),
        in_specs=[pl.BlockSpec((bm, bk), lambda i, k: (i, k)),
                  pl.BlockSpec((None, bk, d), lambda i, k: (layer, k, 0)),
                  row, vec, vec],
        out_specs=[row, row],
        out_shape=[jax.ShapeDtypeStruct((s, d), F32), jax.ShapeDtypeStruct((s, d), BF16)],
        compiler_params=_params(("parallel", "arbitrary")),
        name=name,
    )(x16, w16, h32, g, b)


def _ffn_in_kernel(x_ref, wg_ref, wv_ref, o_ref, wg16_ref, wv16_ref):
    @pl.when(pl.program_id(1) == 0)
    def _():
        wg16_ref[...] = wg_ref[...].astype(BF16)
        wv16_ref[...] = wv_ref[...].astype(BF16)

    bm = o_ref.shape[0]
    rg = min(bm, IN_PROJ_ROW_GROUP)
    for r in range(bm // rg):
        rows = slice(r * rg, (r + 1) * rg)
        x = x_ref[rows, :]
        gate = _dot(x, wg16_ref[...])
        val = _dot(x, wv16_ref[...])
        o_ref[rows, :] = (gate * _sigmoid(gate) * val).astype(BF16)


def _ffn_in(h16, w, layer, bm=2048, bn=512):
    s, k = h16.shape
    bm = min(bm, s)
    nb = D_FF // bn
    return pl.pallas_call(
        _ffn_in_kernel,
        grid=(nb, s // bm),
        in_specs=[pl.BlockSpec((bm, k), lambda j, i: (i, 0)),
                  pl.BlockSpec((None, k, bn), lambda j, i: (layer, 0, j)),
                  pl.BlockSpec((None, k, bn), lambda j, i: (layer, 0, nb + j))],
        out_specs=pl.BlockSpec((bm, bn), lambda j, i: (i, j)),
        out_shape=jax.ShapeDtypeStruct((s, D_FF), BF16),
        scratch_shapes=[pltpu.VMEM((k, bn), BF16), pltpu.VMEM((k, bn), BF16)],
        compiler_params=_params(("parallel", "arbitrary")),
        name="ffn_in",
    )(h16, w, w)


def _rotary_tables(seq):
    half = RET_HEAD_DIM // 2
    inv = ROPE_BASE ** (-jnp.arange(half, dtype=F32) / half)
    ang = jnp.arange(seq, dtype=F32)[:, None] * inv[None, :]
    return jnp.cos(ang), jnp.sin(ang)


def _mixer_branches(qk16, mid16, ret_decay, w_conv, b_conv, lru_wa, lru_ba, lru_wi, lru_bi,
                    lru_lambda, na_rpb):
    decay_lanes = jnp.broadcast_to(ret_decay.reshape(2 * RET_HEADS, 1, 1),
                                   (2 * RET_HEADS, 1, RET_HEAD_DIM))
    ret16 = _retention(qk16, mid16, decay_lanes)
    na16 = _neighborhood_attention(mid16, na_rpb.reshape(-1))
    lru16 = _rglru(mid16, w_conv, b_conv.reshape(1, LRU_WIDTH), lru_wa, lru_ba, lru_wi, lru_bi,
                   lru_lambda)
    return ret16, na16, lru16


def kernel(x, ln_in_g, ln_in_b, w_in, gate_b, ret_decay, w_conv, b_conv, lru_wa, lru_ba, lru_wi, lru_bi, lru_lambda, na_rpb, w_branch, w_out, ln1_g, ln1_b, w_ffn_in, w_ffn_out, ln2_g, ln2_b):
    batch, seq, d = x.shape
    assert batch == 1 and d == D_MODEL
    w_out16 = w_out.astype(BF16)
    w_ffn_out16 = w_ffn_out.astype(BF16)
    gate_b3 = gate_b.reshape(DEPTH, 1, N_BRANCH * d)
    vec3 = lambda v: v.reshape(DEPTH, 1, d)
    ln1_g3, ln1_b3, ln2_g3, ln2_b3 = vec3(ln1_g), vec3(ln1_b), vec3(ln2_g), vec3(ln2_b)
    cos, sin = _rotary_tables(seq)

    h32, h16 = _ln(x.reshape(seq, d), ln_in_g, ln_in_b)
    for l in range(DEPTH):
        qk16 = _in_proj(h16, w_in, l, 0, COL_MID, "rope", (cos, sin))
        mid16 = _in_proj(h16, w_in, l, COL_MID, COL_GATE - COL_MID, "plain")
        gate16 = _in_proj(h16, w_in, l, COL_GATE, IN_COLS - COL_GATE, "gate", (gate_b3,))
        ret16, na16, lru16 = _mixer_branches(qk16, mid16, ret_decay[l], w_conv[l], b_conv[l],
                                             lru_wa[l], lru_ba[l], lru_wi[l], lru_bi[l],
                                             lru_lambda[l], na_rpb[l])
        merged16 = _merge(ret16, na16, lru16, gate16, w_branch, l)
        h32, h16 = _mm_res_ln(merged16, w_out16, l, h32, ln1_g3, ln1_b3, "out_proj", bm=512, bk=D_MODEL)
        act16 = _ffn_in(h16, w_ffn_in, l)
        h32, h16 = _mm_res_ln(act16, w_ffn_out16, l, h32, ln2_g3, ln2_b3, "ffn_out", bm=1024, bk=512)
    return h32.reshape(batch, seq, d)
```

```python
import functools
import math

import jax
import jax.numpy as jnp
from jax import lax
from jax.experimental import pallas as pl
from jax.experimental.pallas import tpu as pltpu

F32 = jnp.float32
BF16 = jnp.bfloat16

D_MODEL = 2048
DEPTH = 4
GRID_W = 64
RET_HEADS = 4
RET_HEAD_DIM = 256
NA_HEADS = 8
NA_HEAD_DIM = 128
NA_KH = 8
NA_KW = 16
LRU_WIDTH = 1024
LRU_BLOCKS = 8
LRU_BLOCK_DIM = 128
LRU_CONV = 4
LRU_C = 8.0
N_BRANCH = 3
BRANCH_WIDTH = 1024
IN_COLS = 15360
D_FF = 5632
DEEPNORM_ALPHA = (2 * DEPTH) ** 0.25
LN_EPS = 1e-5
ROPE_BASE = 10000.0

COL_MID, COL_GATE = 2048, 9216
MID_RV, MID_RG, MID_NQ, MID_NK, MID_NV, MID_LX, MID_LY = (i * 1024 for i in range(7))
IN_PROJ_ROW_GROUP = 256

RET_BLOCK = 256

NA_ROWS_PER_STEP = 8
NA_KEY_ROWS = 16
NA_UNIT_ROWS = 4
NA_UNIT_PAIRS = 6
NA_HEADS_PER_STEP = 2
LOG2_E = math.log2(math.e)
NA_Q_SCALE = NA_HEAD_DIM ** -0.5 * LOG2_E
NA_NEG = -1e30

LRU_TILE = 1024
LRU_HALO = 8

LN_ROW_GROUPS = 4

VMEM_LIMIT = 56 * 1024 * 1024


def _params(sem, vmem=VMEM_LIMIT):
    return pltpu.CompilerParams(dimension_semantics=sem, vmem_limit_bytes=vmem)


def _softplus(x):
    return jnp.maximum(x, 0.0) + jnp.log1p(jnp.exp(-jnp.abs(x)))


def _log_sigmoid(x):
    return -_softplus(-x)


def _sigmoid(x):
    return 0.5 * jnp.tanh(0.5 * x) + 0.5


def _sqrt_nonneg(x):
    return jnp.where(x > 0.0, x * lax.rsqrt(x), 0.0)


def _layer_norm_rows(x, g, b):
    mu = jnp.mean(x, axis=-1, keepdims=True)
    xc = x - mu
    var = jnp.mean(xc * xc, axis=-1, keepdims=True)
    return xc * lax.rsqrt(var + LN_EPS) * g + b


def _dot(a, b):
    return jnp.dot(a, b, preferred_element_type=F32)


def _dot_nt(a, b):
    return lax.dot_general(a, b, (((1,), (1,)), ((), ())), preferred_element_type=F32)


def _dot_tn(a, b):
    return lax.dot_general(a, b, (((0,), (0,)), ((), ())), preferred_element_type=F32)


def _ln_kernel(x_ref, g_ref, b_ref, o32_ref, o16_ref):
    y = _layer_norm_rows(x_ref[...], g_ref[...], b_ref[...])
    o32_ref[...] = y
    o16_ref[...] = y.astype(BF16)


def _ln(x, g, b, bm=512):
    s, d = x.shape
    row = pl.BlockSpec((bm, d), lambda i: (i, 0))
    vec = pl.BlockSpec((1, d), lambda i: (0, 0))
    return pl.pallas_call(
        _ln_kernel,
        grid=(s // bm,),
        in_specs=[row, vec, vec],
        out_specs=[row, row],
        out_shape=[jax.ShapeDtypeStruct((s, d), F32), jax.ShapeDtypeStruct((s, d), BF16)],
        compiler_params=_params(("parallel",)),
        name="ln_in",
    )(x, g.reshape(1, d), b.reshape(1, d))


def _in_proj_kernel(*refs, epilogue):
    if epilogue == "rope":
        x_ref, w_ref, cos_ref, sin_ref, o_ref, w16_ref = refs
    elif epilogue == "gate":
        x_ref, w_ref, gb_ref, o_ref, w16_ref = refs
    else:
        x_ref, w_ref, o_ref, w16_ref = refs
        mid_scale = jnp.where(pl.program_id(0) == MID_NQ // o_ref.shape[1], NA_Q_SCALE, 1.0)

    @pl.when(pl.program_id(1) == 0)
    def _():
        w16_ref[...] = w_ref[...].astype(BF16)

    bm = o_ref.shape[0]
    rg = min(bm, IN_PROJ_ROW_GROUP)
    half = RET_HEAD_DIM // 2
    for r in range(bm // rg):
        rows = slice(r * rg, (r + 1) * rg)
        acc = _dot(x_ref[rows, :], w16_ref[...])
        if epilogue == "rope":
            c = cos_ref[rows, :]
            s = sin_ref[rows, :]
            scale = jnp.where(pl.program_id(0) == 1, RET_HEAD_DIM ** -0.5, 1.0)
            for h in range(RET_HEADS):
                lo = slice(h * RET_HEAD_DIM, h * RET_HEAD_DIM + half)
                hi = slice(h * RET_HEAD_DIM + half, (h + 1) * RET_HEAD_DIM)
                t1, t2 = acc[:, lo], acc[:, hi]
                o_ref[rows, lo] = ((t1 * c - t2 * s) * scale).astype(BF16)
                o_ref[rows, hi] = ((t1 * s + t2 * c) * scale).astype(BF16)
        elif epilogue == "gate":
            o_ref[rows, :] = _sigmoid(acc + gb_ref[...]).astype(BF16)
        else:
            o_ref[rows, :] = (acc * mid_scale).astype(BF16)


def _in_proj(x16, w, layer, col0, ncols, epilogue, extras=(), bm=2048, bn=1024):
    s, k = x16.shape
    bm = min(bm, s)
    in_specs = [pl.BlockSpec((bm, k), lambda j, i: (i, 0)),
                pl.BlockSpec((None, k, bn), lambda j, i: (layer, 0, col0 // bn + j))]
    if epilogue == "rope":
        assert ncols == 2 * RET_HEADS * RET_HEAD_DIM and bn == RET_HEADS * RET_HEAD_DIM
        tab = pl.BlockSpec((bm, RET_HEAD_DIM // 2), lambda j, i: (i, 0))
        in_specs += [tab, tab]
    elif epilogue == "gate":
        in_specs += [pl.BlockSpec((None, 1, bn), lambda j, i: (layer, 0, j))]
    else:
        assert col0 == COL_MID
    return pl.pallas_call(
        functools.partial(_in_proj_kernel, epilogue=epilogue),
        grid=(ncols // bn, s // bm),
        in_specs=in_specs,
        out_specs=pl.BlockSpec((bm, bn), lambda j, i: (i, j)),
        out_shape=jax.ShapeDtypeStruct((s, ncols), BF16),
        scratch_shapes=[pltpu.VMEM((k, bn), BF16)],
        compiler_params=_params(("parallel", "arbitrary")),
        name="in_proj_" + epilogue,
    )(x16, w, *extras)


def _chunk_pos():
    return lax.broadcasted_iota(jnp.int32, (RET_BLOCK, RET_HEAD_DIM), 0).astype(F32)


def _head_cols(h):
    return slice(h * RET_HEAD_DIM, (h + 1) * RET_HEAD_DIM)


def _ret_kernel(q_ref, k_ref, v_ref, g_ref, dl_ref, o_ref,
                sb_ref, sf_ref, qdb_ref, kdb_ref, qdf_ref, kdf_ref, m_ref, ib_ref, *, nc):
    c = RET_BLOCK
    assert c <= RET_HEAD_DIM
    phase = pl.program_id(0)
    n = pl.program_id(1)

    @pl.when(jnp.logical_and(phase == 0, n == 0))
    def _():
        sb_ref[...] = jnp.zeros_like(sb_ref)
        sf_ref[...] = jnp.zeros_like(sf_ref)
        pos = _chunk_pos()
        ri = lax.broadcasted_iota(jnp.int32, (c, c), 0)
        ci = lax.broadcasted_iota(jnp.int32, (c, c), 1)
        diff = (ri - ci).astype(F32)
        for h in range(RET_HEADS):
            lgf = _log_sigmoid(dl_ref[h])
            lgb = _log_sigmoid(dl_ref[RET_HEADS + h])
            qdb_ref[h] = jnp.exp((c - pos) * lgb)
            kdb_ref[h] = jnp.exp(pos * lgb)
            qdf_ref[h] = jnp.exp((pos + 1.0) * lgf)
            kdf_ref[h] = jnp.exp((c - 1.0 - pos) * lgf)
            m_ref[h] = jnp.exp(jnp.abs(diff) * jnp.where(diff >= 0, lgf[:, :c], lgb[:, :c]))

    @pl.when(phase == 0)
    def _():
        rows = pl.ds(pl.multiple_of((nc - 1 - n) * c, c), c)
        for h in range(RET_HEADS):
            cols = _head_cols(h)
            c_dec = jnp.exp(c * _log_sigmoid(dl_ref[RET_HEADS + h]))
            state = sb_ref[h]
            ib_ref[rows, cols] = _dot(q_ref[:, cols], state.astype(BF16)) * qdb_ref[h]
            vk = (v_ref[:, cols].astype(F32) * kdb_ref[h]).astype(BF16)
            sb_ref[h] = c_dec * state + _dot_tn(k_ref[:, cols], vk)

    @pl.when(phase == 1)
    def _():
        rows = pl.ds(pl.multiple_of(n * c, c), c)
        for h in range(RET_HEADS):
            cols = _head_cols(h)
            q = q_ref[:, cols]
            k = k_ref[:, cols]
            v = v_ref[:, cols]
            p = (_dot_nt(q, k) * m_ref[h]).astype(BF16)
            intra = _dot(p, v)
            state = sf_ref[h]
            inter = _dot(q, state.astype(BF16)) * qdf_ref[h]
            y = intra + inter + ib_ref[rows, cols]

            c_dec = jnp.exp(c * _log_sigmoid(dl_ref[h]))
            vk = (v.astype(F32) * kdf_ref[h]).astype(BF16)
            sf_ref[h] = c_dec * state + _dot_tn(k, vk)

            mu = jnp.mean(y, axis=-1, keepdims=True)
            yc = y - mu
            var = jnp.mean(yc * yc, axis=-1, keepdims=True)
            yn = yc * lax.rsqrt(var + LN_EPS)
            g = g_ref[:, cols].astype(F32)
            o_ref[:, cols] = (g * _sigmoid(g) * yn).astype(BF16)


def _retention(qk16, mid16, decay_lanes):
    s = qk16.shape[0]
    c, d, nh = RET_BLOCK, RET_HEAD_DIM, RET_HEADS
    nc = s // c
    w = nh * d
    table = pltpu.VMEM((nh, c, d), F32)

    def chunk(col):
        return pl.BlockSpec((c, w), lambda p, n: (jnp.where(p == 0, nc - 1 - n, n), col))

    def fwd_only(col):
        return pl.BlockSpec((c, w), lambda p, n: (jnp.where(p == 0, 0, n), col))

    return pl.pallas_call(
        functools.partial(_ret_kernel, nc=nc),
        grid=(2, nc),
        in_specs=[chunk(0), chunk(1), chunk(MID_RV // w), fwd_only(MID_RG // w),
                  pl.BlockSpec((2 * nh, 1, d), lambda p, n: (0, 0, 0))],
        out_specs=fwd_only(0),
        out_shape=jax.ShapeDtypeStruct((s, w), BF16),
        scratch_shapes=[pltpu.VMEM((nh, d, d), F32), pltpu.VMEM((nh, d, d), F32),
                        table, table, table, table, pltpu.VMEM((nh, c, c), F32),
                        pltpu.VMEM((s, w), F32)],
        compiler_params=_params(("arbitrary", "arbitrary")),
        name="retention",
    )(qk16, qk16, mid16, mid16, decay_lanes)


_NA_SPAN = NA_KEY_ROWS - NA_KH
_NA_START_PAIR = ((0, 0), (0, 2), (2, 2))


def _na_first_key_row(case, i):
    return (max(i - NA_KH // 2, 0), i, min(i + NA_KH // 2, _NA_SPAN))[case]


def _na_build_bias(rpb_ref, bias_ref, head):
    w = GRID_W
    n_dc = 2 * NA_KW - 1
    n_dr = 2 * NA_KH - 1
    lane = lax.broadcasted_iota(jnp.int32, (w, 2 * w), 1)
    cq = lax.broadcasted_iota(jnp.int32, (w, 2 * w), 0)
    ck = lane & (w - 1)
    dc = jnp.clip(ck - cq, -(NA_KW - 1), NA_KW - 1) + (NA_KW - 1)
    col_start = jnp.clip(cq - NA_KW // 2, 0, w - NA_KW)
    in_win = (ck >= col_start) & (ck < col_start + NA_KW)
    dc_is = [dc == e for e in range(n_dc)]
    neg = jnp.full((w, 2 * w), NA_NEG, F32)
    rows = []
    for dr in range(n_dr):
        t = jnp.zeros((w, 2 * w), F32)
        base = (head * n_dr + dr) * n_dc
        for e in range(n_dc):
            t = jnp.where(dc_is[e], rpb_ref[base + e], t)
        rows.append(jnp.where(in_win, t * LOG2_E, neg))
    left = lane < w
    for case, off in enumerate((0, _NA_SPAN // 2, _NA_SPAN)):
        for u in range(NA_ROWS_PER_STEP // NA_UNIT_ROWS):
            sp = _NA_START_PAIR[case][u]
            for ii in range(NA_UNIT_ROWS):
                i = u * NA_UNIT_ROWS + ii
                first = _na_first_key_row(case, i)
                assert 2 * sp <= first and first + NA_KH <= 2 * (sp + NA_UNIT_PAIRS)

                def tile(j):
                    if first <= j < first + NA_KH:
                        return rows[j - i + (NA_KH - 1) - off]
                    return None

                for jj in range(NA_UNIT_PAIRS):
                    a, b = tile(2 * (sp + jj)), tile(2 * (sp + jj) + 1)
                    if a is None and b is None:
                        blk = neg
                    else:
                        blk = jnp.where(left, neg if a is None else a, neg if b is None else b)
                    bias_ref[case, u, ii * w:(ii + 1) * w, jj * 2 * w:(jj + 1) * 2 * w] = blk


def _na_kernel(rpb_ref, q_ref, k_ref, v_ref, o_ref, bias_ref, *, rows):
    rb = pl.program_id(1)
    nrb = rows // NA_ROWS_PER_STEP
    d = NA_HEAD_DIM
    uq = NA_UNIT_ROWS * GRID_W
    uk = NA_UNIT_PAIRS * 2 * GRID_W

    @pl.when(rb == 0)
    def _():
        for hh in range(NA_HEADS_PER_STEP):
            _na_build_bias(rpb_ref, bias_ref.at[hh], pl.program_id(0) * NA_HEADS_PER_STEP + hh)

    key_row0 = jnp.clip(rb * NA_ROWS_PER_STEP - NA_KH // 2, 0, rows - NA_KEY_ROWS)
    case = jnp.where(rb == 0, 0, jnp.where(rb == nrb - 1, 2, 1))
    for u in range(NA_ROWS_PER_STEP // NA_UNIT_ROWS):
        first, mid, last = (_NA_START_PAIR[cs][u] for cs in range(3))
        sp = jnp.where(rb == 0, first, jnp.where(rb == nrb - 1, last, mid))
        kstart = pl.multiple_of((key_row0 + 2 * sp) * GRID_W, 2 * GRID_W)
        for hh in range(NA_HEADS_PER_STEP):
            cols = slice(hh * d, (hh + 1) * d)
            kk = k_ref[pl.ds(kstart, uk), cols]
            vv = v_ref[pl.ds(kstart, uk), cols]
            q = q_ref[u * uq:(u + 1) * uq, cols]
            s = _dot_nt(q, kk) + bias_ref[hh, case, u]
            m = jnp.max(s, axis=-1, keepdims=True)
            e = jnp.exp2(s - m)
            l = jnp.sum(e, axis=-1, keepdims=True)
            o = _dot(e.astype(BF16), vv) / l
            o_ref[u * uq:(u + 1) * uq, cols] = o.astype(BF16)


def _neighborhood_attention(mid16, rpb_flat):
    s = mid16.shape[0]
    rows = s // GRID_W
    d = NA_HEAD_DIM * NA_HEADS_PER_STEP
    bq = NA_ROWS_PER_STEP * GRID_W
    nrb = rows // NA_ROWS_PER_STEP
    n_units = NA_ROWS_PER_STEP // NA_UNIT_ROWS
    return pl.pallas_call(
        functools.partial(_na_kernel, rows=rows),
        grid=(NA_HEADS // NA_HEADS_PER_STEP, nrb),
        in_specs=[pl.BlockSpec(memory_space=pltpu.SMEM),
                  pl.BlockSpec((bq, d), lambda h, r: (r, MID_NQ // d + h)),
                  pl.BlockSpec((s, d), lambda h, r: (0, MID_NK // d + h)),
                  pl.BlockSpec((s, d), lambda h, r: (0, MID_NV // d + h))],
        out_specs=pl.BlockSpec((bq, d), lambda h, r: (r, h)),
        out_shape=jax.ShapeDtypeStruct((s, NA_HEADS * NA_HEAD_DIM), BF16),
        scratch_shapes=[pltpu.VMEM((NA_HEADS_PER_STEP, 3, n_units, NA_UNIT_ROWS * GRID_W,
                                    NA_UNIT_PAIRS * 2 * GRID_W), F32)],
        compiler_params=_params(("parallel", "arbitrary")),
        name="natten",
    )(rpb_flat, mid16, mid16, mid16)


def _gelu_tanh(x):
    return x * (0.5 * (1.0 + jnp.tanh(math.sqrt(2.0 / math.pi) * (x + 0.044715 * (x * x * x)))))


def _lru_kernel(x_ref, y_ref, wc_ref, bc_ref, waf_ref, wif_ref, wab_ref, wib_ref, ba_ref, bi_ref,
                lam_ref, o_ref, xpad_ref, xc_ref, hf_ref, a_ref, b_ref, hb_ref, *, seq):
    tt = min(LRU_TILE, seq)
    nt = seq // tt
    ng = tt // 8
    lanes = LRU_BLOCK_DIM
    row = lax.broadcasted_iota(jnp.int32, (8, lanes), 0)

    zeros_halo = jnp.zeros((LRU_HALO, lanes), F32)
    xpad_ref[0:LRU_HALO, :] = zeros_halo
    xpad_ref[LRU_HALO + seq:2 * LRU_HALO + seq, :] = zeros_halo

    def stage(t, carry):
        t0 = pl.multiple_of(t * tt, tt)
        xpad_ref[pl.ds(LRU_HALO + t0, tt), :] = x_ref[pl.ds(t0, tt), :].astype(F32)
        return carry

    lax.fori_loop(0, nt, stage, 0)

    def conv_tile(t0):
        base = LRU_HALO - LRU_CONV // 2
        xc = bc_ref[...]
        for j in range(LRU_CONV):
            xc = xc + xpad_ref[pl.ds(t0 + base + j, tt), :] * wc_ref[j:j + 1, :]
        return xc

    def tile_gates(xc, wa_ref, wi_ref, d):
        xb = xc.astype(BF16)
        t_r = jnp.tanh(_dot(xb, (0.5 * wa_ref[...]).astype(BF16)) + 0.5 * ba_ref[d:d + 1, :])
        t_i = jnp.tanh(_dot(xb, (0.5 * wi_ref[...]).astype(BF16)) + 0.5 * bi_ref[d:d + 1, :])
        neg_log_a = (0.5 * LRU_C * _softplus(-lam_ref[d:d + 1, :])) * (t_r + 1.0)
        a = jnp.exp(-neg_log_a)
        a_ref[...] = a
        gain = _sqrt_nonneg(jnp.tanh(neg_log_a) * (a * a + 1.0))
        b_ref[...] = gain * ((0.5 * t_i + 0.5) * xc)

    def fwd_tile(t, carry):
        t0 = pl.multiple_of(t * tt, tt)
        xc = conv_tile(t0)
        xc_ref[pl.ds(t0, tt), :] = xc
        tile_gates(xc, waf_ref, wif_ref, 0)

        def group(gi, c):
            r0 = pl.multiple_of(gi * 8, 8)
            a = a_ref[pl.ds(r0, 8), :]
            b = b_ref[pl.ds(r0, 8), :]
            for sh in (1, 2, 4):
                keep = row >= sh
                a_sh = jnp.where(keep, pltpu.roll(a, sh, 0), 1.0)
                b_sh = jnp.where(keep, pltpu.roll(b, sh, 0), 0.0)
                b = a * b_sh + b
                a = a * a_sh
            h = a * c + b
            hf_ref[pl.ds(t0 + r0, 8), :] = h
            return jnp.broadcast_to(h[7:8, :], (8, lanes))

        return lax.fori_loop(0, ng, group, carry, unroll=8)

    lax.fori_loop(0, nt, fwd_tile, jnp.zeros((8, lanes), F32))

    def bwd_tile(ti, carry):
        t0 = pl.multiple_of((nt - 1 - ti) * tt, tt)
        tile_gates(xc_ref[pl.ds(t0, tt), :], wab_ref, wib_ref, 1)

        def group(gj, c):
            r0 = pl.multiple_of((ng - 1 - gj) * 8, 8)
            a = a_ref[pl.ds(r0, 8), :]
            b = b_ref[pl.ds(r0, 8), :]
            for sh in (1, 2, 4):
                keep = row < 8 - sh
                a_sh = jnp.where(keep, pltpu.roll(a, 8 - sh, 0), 1.0)
                b_sh = jnp.where(keep, pltpu.roll(b, 8 - sh, 0), 0.0)
                b = a * b_sh + b
                a = a * a_sh
            h = a * c + b
            hb_ref[pl.ds(r0, 8), :] = h
            return jnp.broadcast_to(h[0:1, :], (8, lanes))

        carry = lax.fori_loop(0, ng, group, carry, unroll=8)
        h = hf_ref[pl.ds(t0, tt), :] + hb_ref[...]
        y = y_ref[pl.ds(t0, tt), :].astype(F32)
        o_ref[pl.ds(t0, tt), :] = (h * _gelu_tanh(y)).astype(BF16)
        return carry

    lax.fori_loop(0, nt, bwd_tile, jnp.zeros((8, lanes), F32))


def _rglru(mid16, w_conv, b_conv, wa, ba, wi, bi, lam):
    s = mid16.shape[0]
    bd = LRU_BLOCK_DIM
    tt = min(LRU_TILE, s)
    col = lambda off: pl.BlockSpec((s, bd), lambda g: (0, off // bd + g))
    gate_w = lambda d: pl.BlockSpec((None, None, bd, bd), lambda g: (d, g, 0, 0))
    vec2 = pl.BlockSpec((2, bd), lambda g: (0, g))
    return pl.pallas_call(
        functools.partial(_lru_kernel, seq=s),
        grid=(LRU_BLOCKS,),
        in_specs=[col(MID_LX), col(MID_LY),
                  pl.BlockSpec((LRU_CONV, bd), lambda g: (0, g)),
                  pl.BlockSpec((1, bd), lambda g: (0, g)),
                  gate_w(0), gate_w(0), gate_w(1), gate_w(1),
                  vec2, vec2, vec2],
        out_specs=pl.BlockSpec((s, bd), lambda g: (0, g)),
        out_shape=jax.ShapeDtypeStruct((s, LRU_WIDTH), BF16),
        scratch_shapes=[pltpu.VMEM((s + 2 * LRU_HALO, bd), F32),
                        pltpu.VMEM((s, bd), F32),
                        pltpu.VMEM((s, bd), F32),
                        pltpu.VMEM((tt, bd), F32),
                        pltpu.VMEM((tt, bd), F32),
                        pltpu.VMEM((tt, bd), F32)],
        compiler_params=_params(("parallel",)),
        name="rglru",
    )(mid16, mid16, w_conv, b_conv, wa, wi, wa, wi, ba, bi, lam)


def _merge_kernel(r_ref, n_ref, l_ref, wr_ref, wn_ref, wl_ref, gr_ref, gn_ref, gl_ref, o_ref,
                  wr16_ref, wn16_ref, wl16_ref):
    @pl.when(pl.program_id(1) == 0)
    def _():
        for w_ref, w16_ref in ((wr_ref, wr16_ref), (wn_ref, wn16_ref), (wl_ref, wl16_ref)):
            w16_ref[...] = w_ref[...].astype(BF16)

    acc = None
    for x_ref, w16_ref, g_ref in ((r_ref, wr16_ref, gr_ref), (n_ref, wn16_ref, gn_ref),
                                  (l_ref, wl16_ref, gl_ref)):
        term = g_ref[...].astype(F32) * _dot(x_ref[...], w16_ref[...])
        acc = term if acc is None else acc + term
    o_ref[...] = acc.astype(BF16)


def _merge(ret16, na16, lru16, gate16, w_branch, layer, bm=512, bn=1024):
    s = ret16.shape[0]
    d = D_MODEL
    kw = BRANCH_WIDTH
    xin = pl.BlockSpec((bm, kw), lambda j, i: (i, 0))
    wsp = lambda b: pl.BlockSpec((None, None, kw, bn), lambda j, i: (layer, b, 0, j))
    gsp = lambda b: pl.BlockSpec((bm, bn), lambda j, i: (i, b * d // bn + j))
    return pl.pallas_call(
        _merge_kernel,
        grid=(d // bn, s // bm),
        in_specs=[xin, xin, xin, wsp(0), wsp(1), wsp(2), gsp(0), gsp(1), gsp(2)],
        out_specs=pl.BlockSpec((bm, bn), lambda j, i: (i, j)),
        out_shape=jax.ShapeDtypeStruct((s, d), BF16),
        scratch_shapes=[pltpu.VMEM((kw, bn), BF16)] * N_BRANCH,
        compiler_params=_params(("parallel", "arbitrary")),
        name="merge",
    )(ret16, na16, lru16, w_branch, w_branch, w_branch, gate16, gate16, gate16)


def _mm_res_ln_kernel(x_ref, w_ref, h_ref, g_ref, b_ref, o32_ref, o16_ref, *, nk):
    k = pl.program_id(1)
    bm = o32_ref.shape[0]

    def finish(partial):
        rg = bm // LN_ROW_GROUPS
        for rows in (slice(r * rg, (r + 1) * rg) for r in range(LN_ROW_GROUPS)):
            y = _layer_norm_rows(partial(rows) + _dot(x_ref[rows, :], w_ref[...]), g_ref[...], b_ref[...])
            o32_ref[rows, :] = y
            o16_ref[rows, :] = y.astype(BF16)

    if nk == 1:
        finish(lambda rows: DEEPNORM_ALPHA * h_ref[rows, :])
        return

    @pl.when(k == 0)
    def _():
        o32_ref[...] = DEEPNORM_ALPHA * h_ref[...] + _dot(x_ref[...], w_ref[...])

    @pl.when(jnp.logical_and(k > 0, k < nk - 1))
    def _():
        o32_ref[...] += _dot(x_ref[...], w_ref[...])

    @pl.when(k == nk - 1)
    def _():
        finish(lambda rows: o32_ref[rows, :])


def _mm_res_ln(x16, w16, layer, h32, g, b, name, bm, bk):
    s, kdim = x16.shape
    d = D_MODEL
    row = pl.BlockSpec((bm, d), lambda i, k: (i, 0))
    vec = pl.BlockSpec((None, 1, d), lambda i, k: (layer, 0, 0))
    return pl.pallas_call(
        functools.partial(_mm_res_ln_kernel, nk=kdim // bk),
        grid=(s // bm, kdim // bk),
        in_specs=[pl.BlockSpec((bm, bk), lambda i, k: (i, k)),
                  pl.BlockSpec((None, bk, d), lambda i, k: (layer, k, 0)),
                  row, vec, vec],
        out_specs=[row, row],
        out_shape=[jax.ShapeDtypeStruct((s, d), F32), jax.ShapeDtypeStruct((s, d), BF16)],
        compiler_params=_params(("parallel", "arbitrary")),
        name=name,
    )(x16, w16, h32, g, b)


def _ffn_in_kernel(x_ref, wg_ref, wv_ref, o_ref, wg16_ref, wv16_ref):
    @pl.when(pl.program_id(1) == 0)
    def _():
        wg16_ref[...] = wg_ref[...].astype(BF16)
        wv16_ref[...] = wv_ref[...].astype(BF16)

    bm = o_ref.shape[0]
    rg = min(bm, IN_PROJ_ROW_GROUP)
    for r in range(bm // rg):
        rows = slice(r * rg, (r + 1) * rg)
        x = x_ref[rows, :]
        gate = _dot(x, wg16_ref[...])
        val = _dot(x, wv16_ref[...])
        o_ref[rows, :] = (gate * _sigmoid(gate) * val).astype(BF16)


def _ffn_in(h16, w, layer, bm=2048, bn=512):
    s, k = h16.shape
    bm = min(bm, s)
    nb = D_FF // bn
    return pl.pallas_call(
        _ffn_in_kernel,
        grid=(nb, s // bm),
        in_specs=[pl.BlockSpec((bm, k), lambda j, i: (i, 0)),
                  pl.BlockSpec((None, k, bn), lambda j, i: (layer, 0, j)),
                  pl.BlockSpec((None, k, bn), lambda j, i: (layer, 0, nb + j))],
        out_specs=pl.BlockSpec((bm, bn), lambda j, i: (i, j)),
        out_shape=jax.ShapeDtypeStruct((s, D_FF), BF16),
        scratch_shapes=[pltpu.VMEM((k, bn), BF16), pltpu.VMEM((k, bn), BF16)],
        compiler_params=_params(("parallel", "arbitrary")),
        name="ffn_in",
    )(h16, w, w)


def _rotary_tables(seq):
    half = RET_HEAD_DIM // 2
    inv = ROPE_BASE ** (-jnp.arange(half, dtype=F32) / half)
    ang = jnp.arange(seq, dtype=F32)[:, None] * inv[None, :]
    return jnp.cos(ang), jnp.sin(ang)


def _mixer_branches(qk16, mid16, ret_decay, w_conv, b_conv, lru_wa, lru_ba, lru_wi, lru_bi,
                    lru_lambda, na_rpb):
    decay_lanes = jnp.broadcast_to(ret_decay.reshape(2 * RET_HEADS, 1, 1),
                                   (2 * RET_HEADS, 1, RET_HEAD_DIM))
    ret16 = _retention(qk16, mid16, decay_lanes)
    na16 = _neighborhood_attention(mid16, na_rpb.reshape(-1))
    lru16 = _rglru(mid16, w_conv, b_conv.reshape(1, LRU_WIDTH), lru_wa, lru_ba, lru_wi, lru_bi,
                   lru_lambda)
    return ret16, na16, lru16


def kernel(x, ln_in_g, ln_in_b, w_in, gate_b, ret_decay, w_conv, b_conv, lru_wa, lru_ba, lru_wi, lru_bi, lru_lambda, na_rpb, w_branch, w_out, ln1_g, ln1_b, w_ffn_in, w_ffn_out, ln2_g, ln2_b):
    batch, seq, d = x.shape
    assert batch == 1 and d == D_MODEL
    w_out16 = w_out.astype(BF16)
    w_ffn_out16 = w_ffn_out.astype(BF16)
    gate_b3 = gate_b.reshape(DEPTH, 1, N_BRANCH * d)
    vec3 = lambda v: v.reshape(DEPTH, 1, d)
    ln1_g3, ln1_b3, ln2_g3, ln2_b3 = vec3(ln1_g), vec3(ln1_b), vec3(ln2_g), vec3(ln2_b)
    cos, sin = _rotary_tables(seq)

    h32, h16 = _ln(x.reshape(seq, d), ln_in_g, ln_in_b)
    for l in range(DEPTH):
        qk16 = _in_proj(h16, w_in, l, 0, COL_MID, "rope", (cos, sin))
        mid16 = _in_proj(h16, w_in, l, COL_MID, COL_GATE - COL_MID, "mid")
        gate16 = _in_proj(h16, w_in, l, COL_GATE, IN_COLS - COL_GATE, "gate", (gate_b3,))
        ret16, na16, lru16 = _mixer_branches(qk16, mid16, ret_decay[l], w_conv[l], b_conv[l],
                                             lru_wa[l], lru_ba[l], lru_wi[l], lru_bi[l],
                                             lru_lambda[l], na_rpb[l])
        merged16 = _merge(ret16, na16, lru16, gate16, w_branch, l)
        h32, h16 = _mm_res_ln(merged16, w_out16, l, h32, ln1_g3, ln1_b3, "out_proj", bm=512, bk=D_MODEL)
        act16 = _ffn_in(h16, w_ffn_in, l)
        h32, h16 = _mm_res_ln(act16, w_ffn_out16, l, h32, ln2_g3, ln2_b3, "ffn_out", bm=512, bk=D_FF // 2)
    return h32.reshape(batch, seq, d)
```

```python
import functools
import math

import jax
import jax.numpy as jnp
from jax import lax
from jax.experimental import pallas as pl
from jax.experimental.pallas import tpu as pltpu

F32 = jnp.float32
BF16 = jnp.bfloat16

D_MODEL = 2048
DEPTH = 4
GRID_W = 64
RET_HEADS = 4
RET_HEAD_DIM = 256
NA_HEADS = 8
NA_HEAD_DIM = 128
NA_KH = 8
NA_KW = 16
LRU_WIDTH = 1024
LRU_BLOCKS = 8
LRU_BLOCK_DIM = 128
LRU_CONV = 4
LRU_C = 8.0
N_BRANCH = 3
BRANCH_WIDTH = 1024
IN_COLS = 15360
D_FF = 5632
DEEPNORM_ALPHA = (2 * DEPTH) ** 0.25
LN_EPS = 1e-5
ROPE_BASE = 10000.0

COL_RQ, COL_RK, COL_RV, COL_RG = 0, 1024, 2048, 3072
COL_NQ, COL_NK, COL_NV = 4096, 5120, 6144
COL_LX, COL_LY, COL_GATE = 7168, 8192, 9216
IN_PROJ_BN = 1024
IN_PROJ_ROW_GROUP = 256

RET_BLOCK = 256

NA_ROWS_PER_STEP = 8
NA_KEY_ROWS = 16
NA_UNIT_ROWS = 4
NA_UNIT_PAIRS = 6
NA_HEADS_PER_STEP = 2
LOG2_E = math.log2(math.e)
NA_Q_SCALE = NA_HEAD_DIM ** -0.5 * LOG2_E
NA_NEG = -1e30

LRU_TILE = 1024
LRU_HALO = 8

LN_ROW_GROUPS = 4

VMEM_LIMIT = 56 * 1024 * 1024


def _params(sem, vmem=VMEM_LIMIT):
    return pltpu.CompilerParams(dimension_semantics=sem, vmem_limit_bytes=vmem)


def _softplus(x):
    return jnp.maximum(x, 0.0) + jnp.log1p(jnp.exp(-jnp.abs(x)))


def _log_sigmoid(x):
    return -_softplus(-x)


def _sigmoid(x):
    return 0.5 * jnp.tanh(0.5 * x) + 0.5


def _sqrt_nonneg(x):
    return jnp.where(x > 0.0, x * lax.rsqrt(x), 0.0)


def _layer_norm_rows(x, g, b):
    mu = jnp.mean(x, axis=-1, keepdims=True)
    xc = x - mu
    var = jnp.mean(xc * xc, axis=-1, keepdims=True)
    return xc * lax.rsqrt(var + LN_EPS) * g + b


def _dot(a, b):
    return jnp.dot(a, b, preferred_element_type=F32)


def _dot_nt(a, b):
    return lax.dot_general(a, b, (((1,), (1,)), ((), ())), preferred_element_type=F32)


def _dot_tn(a, b):
    return lax.dot_general(a, b, (((0,), (0,)), ((), ())), preferred_element_type=F32)


def _ln_kernel(x_ref, g_ref, b_ref, o32_ref, o16_ref):
    y = _layer_norm_rows(x_ref[...], g_ref[...], b_ref[...])
    o32_ref[...] = y
    o16_ref[...] = y.astype(BF16)


def _ln(x, g, b, bm=512):
    s, d = x.shape
    row = pl.BlockSpec((bm, d), lambda i: (i, 0))
    vec = pl.BlockSpec((1, d), lambda i: (0, 0))
    return pl.pallas_call(
        _ln_kernel,
        grid=(s // bm,),
        in_specs=[row, vec, vec],
        out_specs=[row, row],
        out_shape=[jax.ShapeDtypeStruct((s, d), F32), jax.ShapeDtypeStruct((s, d), BF16)],
        compiler_params=_params(("parallel",)),
        name="ln_in",
    )(x, g.reshape(1, d), b.reshape(1, d))


def _in_proj_kernel(x_ref, w_ref, cos_ref, sin_ref, gb_ref, o_ref, w16_ref):
    col_block = pl.program_id(0)
    bm, bn = o_ref.shape
    assert bn == RET_HEADS * RET_HEAD_DIM
    rg = min(bm, IN_PROJ_ROW_GROUP)
    half = RET_HEAD_DIM // 2

    @pl.when(pl.program_id(1) == 0)
    def _():
        w16_ref[...] = w_ref[...].astype(BF16)

    def tile(epilogue):
        for r in range(bm // rg):
            rows = slice(r * rg, (r + 1) * rg)
            acc = _dot(x_ref[rows, :], w16_ref[...])
            if epilogue == "rope":
                c = cos_ref[rows, :]
                s = sin_ref[rows, :]
                scale = jnp.where(col_block == COL_RK // bn, RET_HEAD_DIM ** -0.5, 1.0)
                for h in range(RET_HEADS):
                    lo = slice(h * RET_HEAD_DIM, h * RET_HEAD_DIM + half)
                    hi = slice(h * RET_HEAD_DIM + half, (h + 1) * RET_HEAD_DIM)
                    t1, t2 = acc[:, lo], acc[:, hi]
                    o_ref[rows, lo] = ((t1 * c - t2 * s) * scale).astype(BF16)
                    o_ref[rows, hi] = ((t1 * s + t2 * c) * scale).astype(BF16)
            elif epilogue == "gate":
                o_ref[rows, :] = _sigmoid(acc + gb_ref[...]).astype(BF16)
            else:
                scale = jnp.where(col_block == COL_NQ // bn, NA_Q_SCALE, 1.0)
                o_ref[rows, :] = (acc * scale).astype(BF16)

    pl.when(col_block < COL_RV // bn)(functools.partial(tile, "rope"))
    pl.when(jnp.logical_and(col_block >= COL_RV // bn, col_block < COL_GATE // bn))(
        functools.partial(tile, "mid"))
    pl.when(col_block >= COL_GATE // bn)(functools.partial(tile, "gate"))


def _in_proj(x16, w, layer, cos, sin, gate_b3, bm=2048):
    s, k = x16.shape
    bm = min(bm, s)
    bn = IN_PROJ_BN
    first_gate = COL_GATE // bn
    tab = pl.BlockSpec((bm, RET_HEAD_DIM // 2), lambda j, i: (i, 0))
    return pl.pallas_call(
        _in_proj_kernel,
        grid=(IN_COLS // bn, s // bm),
        in_specs=[pl.BlockSpec((bm, k), lambda j, i: (i, 0)),
                  pl.BlockSpec((None, k, bn), lambda j, i: (layer, 0, j)),
                  tab, tab,
                  pl.BlockSpec((None, 1, bn), lambda j, i: (layer, 0, jnp.maximum(j - first_gate, 0)))],
        out_specs=pl.BlockSpec((bm, bn), lambda j, i: (i, j)),
        out_shape=jax.ShapeDtypeStruct((s, IN_COLS), BF16),
        scratch_shapes=[pltpu.VMEM((k, bn), BF16)],
        compiler_params=_params(("parallel", "arbitrary")),
        name="in_proj",
    )(x16, w, cos, sin, gate_b3)


def _chunk_pos():
    return lax.broadcasted_iota(jnp.int32, (RET_BLOCK, RET_HEAD_DIM), 0).astype(F32)


def _head_cols(h):
    return slice(h * RET_HEAD_DIM, (h + 1) * RET_HEAD_DIM)


def _ret_kernel(q_ref, k_ref, v_ref, g_ref, dl_ref, o_ref,
                sb_ref, sf_ref, qdb_ref, kdb_ref, qdf_ref, kdf_ref, m_ref, ib_ref, *, nc):
    c = RET_BLOCK
    assert c <= RET_HEAD_DIM
    phase = pl.program_id(0)
    n = pl.program_id(1)

    @pl.when(jnp.logical_and(phase == 0, n == 0))
    def _():
        sb_ref[...] = jnp.zeros_like(sb_ref)
        sf_ref[...] = jnp.zeros_like(sf_ref)
        pos = _chunk_pos()
        ri = lax.broadcasted_iota(jnp.int32, (c, c), 0)
        ci = lax.broadcasted_iota(jnp.int32, (c, c), 1)
        diff = (ri - ci).astype(F32)
        for h in range(RET_HEADS):
            lgf = _log_sigmoid(dl_ref[h])
            lgb = _log_sigmoid(dl_ref[RET_HEADS + h])
            qdb_ref[h] = jnp.exp((c - pos) * lgb)
            kdb_ref[h] = jnp.exp(pos * lgb)
            qdf_ref[h] = jnp.exp((pos + 1.0) * lgf)
            kdf_ref[h] = jnp.exp((c - 1.0 - pos) * lgf)
            m_ref[h] = jnp.exp(jnp.abs(diff) * jnp.where(diff >= 0, lgf[:, :c], lgb[:, :c]))

    @pl.when(phase == 0)
    def _():
        rows = pl.ds(pl.multiple_of((nc - 1 - n) * c, c), c)
        for h in range(RET_HEADS):
            cols = _head_cols(h)
            c_dec = jnp.exp(c * _log_sigmoid(dl_ref[RET_HEADS + h]))
            state = sb_ref[h]
            ib_ref[rows, cols] = _dot(q_ref[:, cols], state.astype(BF16)) * qdb_ref[h]
            vk = (v_ref[:, cols].astype(F32) * kdb_ref[h]).astype(BF16)
            sb_ref[h] = c_dec * state + _dot_tn(k_ref[:, cols], vk)

    @pl.when(phase == 1)
    def _():
        rows = pl.ds(pl.multiple_of(n * c, c), c)
        for h in range(RET_HEADS):
            cols = _head_cols(h)
            q = q_ref[:, cols]
            k = k_ref[:, cols]
            v = v_ref[:, cols]
            p = (_dot_nt(q, k) * m_ref[h]).astype(BF16)
            intra = _dot(p, v)
            state = sf_ref[h]
            inter = _dot(q, state.astype(BF16)) * qdf_ref[h]
            y = intra + inter + ib_ref[rows, cols]

            c_dec = jnp.exp(c * _log_sigmoid(dl_ref[h]))
            vk = (v.astype(F32) * kdf_ref[h]).astype(BF16)
            sf_ref[h] = c_dec * state + _dot_tn(k, vk)

            mu = jnp.mean(y, axis=-1, keepdims=True)
            yc = y - mu
            var = jnp.mean(yc * yc, axis=-1, keepdims=True)
            yn = yc * lax.rsqrt(var + LN_EPS)
            g = g_ref[:, cols].astype(F32)
            o_ref[:, cols] = (g * _sigmoid(g) * yn).astype(BF16)


def _retention(proj16, decay_lanes):
    s = proj16.shape[0]
    c, d, nh = RET_BLOCK, RET_HEAD_DIM, RET_HEADS
    nc = s // c
    w = nh * d
    table = pltpu.VMEM((nh, c, d), F32)

    def chunk(col):
        return pl.BlockSpec((c, w), lambda p, n: (jnp.where(p == 0, nc - 1 - n, n), col))

    def fwd_only(col):
        return pl.BlockSpec((c, w), lambda p, n: (jnp.where(p == 0, 0, n), col))

    return pl.pallas_call(
        functools.partial(_ret_kernel, nc=nc),
        grid=(2, nc),
        in_specs=[chunk(COL_RQ // w), chunk(COL_RK // w), chunk(COL_RV // w), fwd_only(COL_RG // w),
                  pl.BlockSpec((2 * nh, 1, d), lambda p, n: (0, 0, 0))],
        out_specs=fwd_only(0),
        out_shape=jax.ShapeDtypeStruct((s, w), BF16),
        scratch_shapes=[pltpu.VMEM((nh, d, d), F32), pltpu.VMEM((nh, d, d), F32),
                        table, table, table, table, pltpu.VMEM((nh, c, c), F32),
                        pltpu.VMEM((s, w), F32)],
        compiler_params=_params(("arbitrary", "arbitrary")),
        name="retention",
    )(proj16, proj16, proj16, proj16, decay_lanes)


_NA_SPAN = NA_KEY_ROWS - NA_KH
_NA_START_PAIR = ((0, 0), (0, 2), (2, 2))


def _na_first_key_row(case, i):
    return (max(i - NA_KH // 2, 0), i, min(i + NA_KH // 2, _NA_SPAN))[case]


def _na_build_bias(rpb_ref, bias_ref, head):
    w = GRID_W
    n_dc = 2 * NA_KW - 1
    n_dr = 2 * NA_KH - 1
    lane = lax.broadcasted_iota(jnp.int32, (w, 2 * w), 1)
    cq = lax.broadcasted_iota(jnp.int32, (w, 2 * w), 0)
    ck = lane & (w - 1)
    dc = jnp.clip(ck - cq, -(NA_KW - 1), NA_KW - 1) + (NA_KW - 1)
    col_start = jnp.clip(cq - NA_KW // 2, 0, w - NA_KW)
    in_win = (ck >= col_start) & (ck < col_start + NA_KW)
    dc_is = [dc == e for e in range(n_dc)]
    neg = jnp.full((w, 2 * w), NA_NEG, F32)
    rows = []
    for dr in range(n_dr):
        t = jnp.zeros((w, 2 * w), F32)
        base = (head * n_dr + dr) * n_dc
        for e in range(n_dc):
            t = jnp.where(dc_is[e], rpb_ref[base + e], t)
        rows.append(jnp.where(in_win, t * LOG2_E, neg))
    left = lane < w
    for case, off in enumerate((0, _NA_SPAN // 2, _NA_SPAN)):
        for u in range(NA_ROWS_PER_STEP // NA_UNIT_ROWS):
            sp = _NA_START_PAIR[case][u]
            for ii in range(NA_UNIT_ROWS):
                i = u * NA_UNIT_ROWS + ii
                first = _na_first_key_row(case, i)
                assert 2 * sp <= first and first + NA_KH <= 2 * (sp + NA_UNIT_PAIRS)

                def tile(j):
                    if first <= j < first + NA_KH:
                        return rows[j - i + (NA_KH - 1) - off]
                    return None

                for jj in range(NA_UNIT_PAIRS):
                    a, b = tile(2 * (sp + jj)), tile(2 * (sp + jj) + 1)
                    if a is None and b is None:
                        blk = neg
                    else:
                        blk = jnp.where(left, neg if a is None else a, neg if b is None else b)
                    bias_ref[case, u, ii * w:(ii + 1) * w, jj * 2 * w:(jj + 1) * 2 * w] = blk


def _na_kernel(rpb_ref, q_ref, k_ref, v_ref, o_ref, bias_ref, *, rows):
    rb = pl.program_id(1)
    nrb = rows // NA_ROWS_PER_STEP
    d = NA_HEAD_DIM
    uq = NA_UNIT_ROWS * GRID_W
    uk = NA_UNIT_PAIRS * 2 * GRID_W

    @pl.when(rb == 0)
    def _():
        for hh in range(NA_HEADS_PER_STEP):
            _na_build_bias(rpb_ref, bias_ref.at[hh], pl.program_id(0) * NA_HEADS_PER_STEP + hh)

    key_row0 = jnp.clip(rb * NA_ROWS_PER_STEP - NA_KH // 2, 0, rows - NA_KEY_ROWS)
    case = jnp.where(rb == 0, 0, jnp.where(rb == nrb - 1, 2, 1))
    for u in range(NA_ROWS_PER_STEP // NA_UNIT_ROWS):
        first, mid, last = (_NA_START_PAIR[cs][u] for cs in range(3))
        sp = jnp.where(rb == 0, first, jnp.where(rb == nrb - 1, last, mid))
        kstart = pl.multiple_of((key_row0 + 2 * sp) * GRID_W, 2 * GRID_W)
        for hh in range(NA_HEADS_PER_STEP):
            cols = slice(hh * d, (hh + 1) * d)
            kk = k_ref[pl.ds(kstart, uk), cols]
            vv = v_ref[pl.ds(kstart, uk), cols]
            q = q_ref[u * uq:(u + 1) * uq, cols]
            s = _dot_nt(q, kk) + bias_ref[hh, case, u]
            m = jnp.max(s, axis=-1, keepdims=True)
            e = jnp.exp2(s - m)
            l = jnp.sum(e, axis=-1, keepdims=True)
            o = _dot(e.astype(BF16), vv) / l
            o_ref[u * uq:(u + 1) * uq, cols] = o.astype(BF16)


def _neighborhood_attention(proj16, rpb_flat):
    s = proj16.shape[0]
    rows = s // GRID_W
    d = NA_HEAD_DIM * NA_HEADS_PER_STEP
    bq = NA_ROWS_PER_STEP * GRID_W
    nrb = rows // NA_ROWS_PER_STEP
    n_units = NA_ROWS_PER_STEP // NA_UNIT_ROWS
    return pl.pallas_call(
        functools.partial(_na_kernel, rows=rows),
        grid=(NA_HEADS // NA_HEADS_PER_STEP, nrb),
        in_specs=[pl.BlockSpec(memory_space=pltpu.SMEM),
                  pl.BlockSpec((bq, d), lambda h, r: (r, COL_NQ // d + h)),
                  pl.BlockSpec((s, d), lambda h, r: (0, COL_NK // d + h)),
                  pl.BlockSpec((s, d), lambda h, r: (0, COL_NV // d + h))],
        out_specs=pl.BlockSpec((bq, d), lambda h, r: (r, h)),
        out_shape=jax.ShapeDtypeStruct((s, NA_HEADS * NA_HEAD_DIM), BF16),
        scratch_shapes=[pltpu.VMEM((NA_HEADS_PER_STEP, 3, n_units, NA_UNIT_ROWS * GRID_W,
                                    NA_UNIT_PAIRS * 2 * GRID_W), F32)],
        compiler_params=_params(("parallel", "arbitrary")),
        name="natten",
    )(rpb_flat, proj16, proj16, proj16)


def _gelu_tanh(x):
    return x * (0.5 * (1.0 + jnp.tanh(math.sqrt(2.0 / math.pi) * (x + 0.044715 * (x * x * x)))))


def _lru_kernel(x_ref, y_ref, wc_ref, bc_ref, waf_ref, wif_ref, wab_ref, wib_ref, ba_ref, bi_ref,
                lam_ref, o_ref, xpad_ref, xc_ref, hf_ref, a_ref, b_ref, hb_ref, *, seq):
    tt = min(LRU_TILE, seq)
    nt = seq // tt
    ng = tt // 8
    lanes = LRU_BLOCK_DIM
    row = lax.broadcasted_iota(jnp.int32, (8, lanes), 0)

    zeros_halo = jnp.zeros((LRU_HALO, lanes), F32)
    xpad_ref[0:LRU_HALO, :] = zeros_halo
    xpad_ref[LRU_HALO + seq:2 * LRU_HALO + seq, :] = zeros_halo

    def stage(t, carry):
        t0 = pl.multiple_of(t * tt, tt)
        xpad_ref[pl.ds(LRU_HALO + t0, tt), :] = x_ref[pl.ds(t0, tt), :].astype(F32)
        return carry

    lax.fori_loop(0, nt, stage, 0)

    def conv_tile(t0):
        base = LRU_HALO - LRU_CONV // 2
        xc = bc_ref[...]
        for j in range(LRU_CONV):
            xc = xc + xpad_ref[pl.ds(t0 + base + j, tt), :] * wc_ref[j:j + 1, :]
        return xc

    def tile_gates(xc, wa_ref, wi_ref, d):
        xb = xc.astype(BF16)
        t_r = jnp.tanh(_dot(xb, (0.5 * wa_ref[...]).astype(BF16)) + 0.5 * ba_ref[d:d + 1, :])
        t_i = jnp.tanh(_dot(xb, (0.5 * wi_ref[...]).astype(BF16)) + 0.5 * bi_ref[d:d + 1, :])
        neg_log_a = (0.5 * LRU_C * _softplus(-lam_ref[d:d + 1, :])) * (t_r + 1.0)
        a = jnp.exp(-neg_log_a)
        a_ref[...] = a
        gain = _sqrt_nonneg(jnp.tanh(neg_log_a) * (a * a + 1.0))
        b_ref[...] = gain * ((0.5 * t_i + 0.5) * xc)

    def fwd_tile(t, carry):
        t0 = pl.multiple_of(t * tt, tt)
        xc = conv_tile(t0)
        xc_ref[pl.ds(t0, tt), :] = xc
        tile_gates(xc, waf_ref, wif_ref, 0)

        def group(gi, c):
            r0 = pl.multiple_of(gi * 8, 8)
            a = a_ref[pl.ds(r0, 8), :]
            b = b_ref[pl.ds(r0, 8), :]
            for sh in (1, 2, 4):
                keep = row >= sh
                a_sh = jnp.where(keep, pltpu.roll(a, sh, 0), 1.0)
                b_sh = jnp.where(keep, pltpu.roll(b, sh, 0), 0.0)
                b = a * b_sh + b
                a = a * a_sh
            h = a * c + b
            hf_ref[pl.ds(t0 + r0, 8), :] = h
            return jnp.broadcast_to(h[7:8, :], (8, lanes))

        return lax.fori_loop(0, ng, group, carry, unroll=8)

    lax.fori_loop(0, nt, fwd_tile, jnp.zeros((8, lanes), F32))

    def bwd_tile(ti, carry):
        t0 = pl.multiple_of((nt - 1 - ti) * tt, tt)
        tile_gates(xc_ref[pl.ds(t0, tt), :], wab_ref, wib_ref, 1)

        def group(gj, c):
            r0 = pl.multiple_of((ng - 1 - gj) * 8, 8)
            a = a_ref[pl.ds(r0, 8), :]
            b = b_ref[pl.ds(r0, 8), :]
            for sh in (1, 2, 4):
                keep = row < 8 - sh
                a_sh = jnp.where(keep, pltpu.roll(a, 8 - sh, 0), 1.0)
                b_sh = jnp.where(keep, pltpu.roll(b, 8 - sh, 0), 0.0)
                b = a * b_sh + b
                a = a * a_sh
            h = a * c + b
            hb_ref[pl.ds(r0, 8), :] = h
            return jnp.broadcast_to(h[0:1, :], (8, lanes))

        carry = lax.fori_loop(0, ng, group, carry, unroll=8)
        h = hf_ref[pl.ds(t0, tt), :] + hb_ref[...]
        y = y_ref[pl.ds(t0, tt), :].astype(F32)
        o_ref[pl.ds(t0, tt), :] = (h * _gelu_tanh(y)).astype(BF16)
        return carry

    lax.fori_loop(0, nt, bwd_tile, jnp.zeros((8, lanes), F32))


def _rglru(proj16, w_conv, b_conv, wa, ba, wi, bi, lam):
    s = proj16.shape[0]
    bd = LRU_BLOCK_DIM
    tt = min(LRU_TILE, s)
    col = lambda off: pl.BlockSpec((s, bd), lambda g: (0, off // bd + g))
    gate_w = lambda d: pl.BlockSpec((None, None, bd, bd), lambda g: (d, g, 0, 0))
    vec2 = pl.BlockSpec((2, bd), lambda g: (0, g))
    return pl.pallas_call(
        functools.partial(_lru_kernel, seq=s),
        grid=(LRU_BLOCKS,),
        in_specs=[col(COL_LX), col(COL_LY),
                  pl.BlockSpec((LRU_CONV, bd), lambda g: (0, g)),
                  pl.BlockSpec((1, bd), lambda g: (0, g)),
                  gate_w(0), gate_w(0), gate_w(1), gate_w(1),
                  vec2, vec2, vec2],
        out_specs=pl.BlockSpec((s, bd), lambda g: (0, g)),
        out_shape=jax.ShapeDtypeStruct((s, LRU_WIDTH), BF16),
        scratch_shapes=[pltpu.VMEM((s + 2 * LRU_HALO, bd), F32),
                        pltpu.VMEM((s, bd), F32),
                        pltpu.VMEM((s, bd), F32),
                        pltpu.VMEM((tt, bd), F32),
                        pltpu.VMEM((tt, bd), F32),
                        pltpu.VMEM((tt, bd), F32)],
        compiler_params=_params(("parallel",)),
        name="rglru",
    )(proj16, proj16, w_conv, b_conv, wa, wi, wa, wi, ba, bi, lam)


def _merge_kernel(r_ref, n_ref, l_ref, wr_ref, wn_ref, wl_ref, gr_ref, gn_ref, gl_ref, o_ref,
                  wr16_ref, wn16_ref, wl16_ref):
    @pl.when(pl.program_id(1) == 0)
    def _():
        for w_ref, w16_ref in ((wr_ref, wr16_ref), (wn_ref, wn16_ref), (wl_ref, wl16_ref)):
            w16_ref[...] = w_ref[...].astype(BF16)

    acc = None
    for x_ref, w16_ref, g_ref in ((r_ref, wr16_ref, gr_ref), (n_ref, wn16_ref, gn_ref),
                                  (l_ref, wl16_ref, gl_ref)):
        term = g_ref[...].astype(F32) * _dot(x_ref[...], w16_ref[...])
        acc = term if acc is None else acc + term
    o_ref[...] = acc.astype(BF16)


def _merge(ret16, na16, lru16, proj16, w_branch, layer, bm=512, bn=1024):
    s = ret16.shape[0]
    d = D_MODEL
    kw = BRANCH_WIDTH
    xin = pl.BlockSpec((bm, kw), lambda j, i: (i, 0))
    wsp = lambda b: pl.BlockSpec((None, None, kw, bn), lambda j, i: (layer, b, 0, j))
    gsp = lambda b: pl.BlockSpec((bm, bn), lambda j, i: (i, (COL_GATE + b * d) // bn + j))
    return pl.pallas_call(
        _merge_kernel,
        grid=(d // bn, s // bm),
        in_specs=[xin, xin, xin, wsp(0), wsp(1), wsp(2), gsp(0), gsp(1), gsp(2)],
        out_specs=pl.BlockSpec((bm, bn), lambda j, i: (i, j)),
        out_shape=jax.ShapeDtypeStruct((s, d), BF16),
        scratch_shapes=[pltpu.VMEM((kw, bn), BF16)] * N_BRANCH,
        compiler_params=_params(("parallel", "arbitrary")),
        name="merge",
    )(ret16, na16, lru16, w_branch, w_branch, w_branch, proj16, proj16, proj16)


def _mm_res_kernel(x_ref, w_ref, h_ref, o_ref):
    o_ref[...] = DEEPNORM_ALPHA * h_ref[...] + _dot(x_ref[...], w_ref[...])


def _mm_res_ln_kernel(x_ref, w_ref, r_ref, g_ref, b_ref, o32_ref, o16_ref, *, res_scale):
    bm = o32_ref.shape[0]
    rg = bm // LN_ROW_GROUPS
    for rows in (slice(r * rg, (r + 1) * rg) for r in range(LN_ROW_GROUPS)):
        res = r_ref[rows, :] if res_scale == 1.0 else res_scale * r_ref[rows, :]
        y = _layer_norm_rows(res + _dot(x_ref[rows, :], w_ref[...]), g_ref[...], b_ref[...])
        o32_ref[rows, :] = y
        o16_ref[rows, :] = y.astype(BF16)


def _kblock_specs(bm, bk, kblock, layer):
    return [pl.BlockSpec((bm, bk), lambda i: (i, kblock)),
            pl.BlockSpec((None, bk, D_MODEL), lambda i: (layer, kblock, 0))]


def _mm_res(x16, w16, layer, h32, name, bm, bk, kblock):
    s = x16.shape[0]
    row = pl.BlockSpec((bm, D_MODEL), lambda i: (i, 0))
    return pl.pallas_call(
        _mm_res_kernel,
        grid=(s // bm,),
        in_specs=_kblock_specs(bm, bk, kblock, layer) + [row],
        out_specs=row,
        out_shape=jax.ShapeDtypeStruct((s, D_MODEL), F32),
        compiler_params=_params(("parallel",)),
        name=name,
    )(x16, w16, h32)


def _mm_res_ln(x16, w16, layer, res32, res_scale, g, b, name, bm, bk, kblock=0):
    s = x16.shape[0]
    d = D_MODEL
    row = pl.BlockSpec((bm, d), lambda i: (i, 0))
    vec = pl.BlockSpec((None, 1, d), lambda i: (layer, 0, 0))
    return pl.pallas_call(
        functools.partial(_mm_res_ln_kernel, res_scale=res_scale),
        grid=(s // bm,),
        in_specs=_kblock_specs(bm, bk, kblock, layer) + [row, vec, vec],
        out_specs=[row, row],
        out_shape=[jax.ShapeDtypeStruct((s, d), F32), jax.ShapeDtypeStruct((s, d), BF16)],
        compiler_params=_params(("parallel",)),
        name=name,
    )(x16, w16, res32, g, b)


def _ffn_in_kernel(x_ref, wg_ref, wv_ref, o_ref, wg16_ref, wv16_ref):
    @pl.when(pl.program_id(1) == 0)
    def _():
        wg16_ref[...] = wg_ref[...].astype(BF16)
        wv16_ref[...] = wv_ref[...].astype(BF16)

    bm = o_ref.shape[0]
    rg = min(bm, IN_PROJ_ROW_GROUP)
    for r in range(bm // rg):
        rows = slice(r * rg, (r + 1) * rg)
        x = x_ref[rows, :]
        gate = _dot(x, wg16_ref[...])
        val = _dot(x, wv16_ref[...])
        o_ref[rows, :] = (gate * _sigmoid(gate) * val).astype(BF16)


def _ffn_in(h16, w, layer, bm=2048, bn=512):
    s, k = h16.shape
    bm = min(bm, s)
    nb = D_FF // bn
    return pl.pallas_call(
        _ffn_in_kernel,
        grid=(nb, s // bm),
        in_specs=[pl.BlockSpec((bm, k), lambda j, i: (i, 0)),
                  pl.BlockSpec((None, k, bn), lambda j, i: (layer, 0, j)),
                  pl.BlockSpec((None, k, bn), lambda j, i: (layer, 0, nb + j))],
        out_specs=pl.BlockSpec((bm, bn), lambda j, i: (i, j)),
        out_shape=jax.ShapeDtypeStruct((s, D_FF), BF16),
        scratch_shapes=[pltpu.VMEM((k, bn), BF16), pltpu.VMEM((k, bn), BF16)],
        compiler_params=_params(("parallel", "arbitrary")),
        name="ffn_in",
    )(h16, w, w)


def _rotary_tables(seq):
    half = RET_HEAD_DIM // 2
    inv = ROPE_BASE ** (-jnp.arange(half, dtype=F32) / half)
    ang = jnp.arange(seq, dtype=F32)[:, None] * inv[None, :]
    return jnp.cos(ang), jnp.sin(ang)


def _mixer_branches(proj16, ret_decay, w_conv, b_conv, lru_wa, lru_ba, lru_wi, lru_bi,
                    lru_lambda, na_rpb):
    decay_lanes = jnp.broadcast_to(ret_decay.reshape(2 * RET_HEADS, 1, 1),
                                   (2 * RET_HEADS, 1, RET_HEAD_DIM))
    ret16 = _retention(proj16, decay_lanes)
    na16 = _neighborhood_attention(proj16, na_rpb.reshape(-1))
    lru16 = _rglru(proj16, w_conv, b_conv.reshape(1, LRU_WIDTH), lru_wa, lru_ba, lru_wi, lru_bi,
                   lru_lambda)
    return ret16, na16, lru16


def kernel(x, ln_in_g, ln_in_b, w_in, gate_b, ret_decay, w_conv, b_conv, lru_wa, lru_ba, lru_wi, lru_bi, lru_lambda, na_rpb, w_branch, w_out, ln1_g, ln1_b, w_ffn_in, w_ffn_out, ln2_g, ln2_b):
    batch, seq, d = x.shape
    assert batch == 1 and d == D_MODEL
    w_out16 = w_out.astype(BF16)
    w_ffn_out16 = w_ffn_out.astype(BF16)
    gate_b3 = gate_b.reshape(DEPTH, 1, N_BRANCH * d)
    vec3 = lambda v: v.reshape(DEPTH, 1, d)
    ln1_g3, ln1_b3, ln2_g3, ln2_b3 = vec3(ln1_g), vec3(ln1_b), vec3(ln2_g), vec3(ln2_b)
    cos, sin = _rotary_tables(seq)

    h32, h16 = _ln(x.reshape(seq, d), ln_in_g, ln_in_b)
    for l in range(DEPTH):
        proj16 = _in_proj(h16, w_in, l, cos, sin, gate_b3)
        ret16, na16, lru16 = _mixer_branches(proj16, ret_decay[l], w_conv[l], b_conv[l],
                                             lru_wa[l], lru_ba[l], lru_wi[l], lru_bi[l],
                                             lru_lambda[l], na_rpb[l])
        merged16 = _merge(ret16, na16, lru16, proj16, w_branch, l)
        h32, h16 = _mm_res_ln(merged16, w_out16, l, h32, DEEPNORM_ALPHA, ln1_g3, ln1_b3, "out_proj",
                              bm=512, bk=D_MODEL)
        act16 = _ffn_in(h16, w_ffn_in, l)
        part = _mm_res(act16, w_ffn_out16, l, h32, "ffn_out_a", bm=512, bk=D_FF // 2, kblock=0)
        h32, h16 = _mm_res_ln(act16, w_ffn_out16, l, part, 1.0, ln2_g3, ln2_b3, "ffn_out_b",
                              bm=512, bk=D_FF // 2, kblock=1)
    return h32.reshape(batch, seq, d)
```

```python
import functools
import math

import jax
import jax.numpy as jnp
from jax import lax
from jax.experimental import pallas as pl
from jax.experimental.pallas import tpu as pltpu

F32 = jnp.float32
BF16 = jnp.bfloat16

D_MODEL = 2048
DEPTH = 4
GRID_W = 64
RET_HEADS = 4
RET_HEAD_DIM = 256
NA_HEADS = 8
NA_HEAD_DIM = 128
NA_KH = 8
NA_KW = 16
LRU_WIDTH = 1024
LRU_BLOCKS = 8
LRU_BLOCK_DIM = 128
LRU_CONV = 4
LRU_C = 8.0
N_BRANCH = 3
BRANCH_WIDTH = 1024
IN_COLS = 15360
D_FF = 5632
DEEPNORM_ALPHA = (2 * DEPTH) ** 0.25
LN_EPS = 1e-5
ROPE_BASE = 10000.0

COL_RQ, COL_RK, COL_RV, COL_RG = 0, 1024, 2048, 3072
COL_NQ, COL_NK, COL_NV = 4096, 5120, 6144
COL_LX, COL_LY, COL_GATE = 7168, 8192, 9216
IN_PROJ_BN = 1024
IN_PROJ_ROW_GROUP = 256

RET_BLOCK = 256

NA_ROWS_PER_STEP = 8
NA_KEY_ROWS = 16
NA_UNIT_ROWS = 4
NA_UNIT_PAIRS = 6
NA_HEADS_PER_STEP = 2
LOG2_E = math.log2(math.e)
NA_Q_SCALE = NA_HEAD_DIM ** -0.5 * LOG2_E
NA_NEG = -1e30

LRU_TILE = 1024
LRU_HALO = 8

LN_ROW_GROUPS = 4

VMEM_LIMIT = 56 * 1024 * 1024


def _params(sem, vmem=VMEM_LIMIT):
    return pltpu.CompilerParams(dimension_semantics=sem, vmem_limit_bytes=vmem)


def _softplus(x):
    return jnp.maximum(x, 0.0) + jnp.log1p(jnp.exp(-jnp.abs(x)))


def _log_sigmoid(x):
    return -_softplus(-x)


def _sigmoid(x):
    return 0.5 * jnp.tanh(0.5 * x) + 0.5


def _sqrt_nonneg(x):
    return jnp.where(x > 0.0, x * lax.rsqrt(x), 0.0)


def _layer_norm_rows(x, g, b):
    mu = jnp.mean(x, axis=-1, keepdims=True)
    xc = x - mu
    var = jnp.mean(xc * xc, axis=-1, keepdims=True)
    return xc * lax.rsqrt(var + LN_EPS) * g + b


def _dot(a, b):
    return jnp.dot(a, b, preferred_element_type=F32)


def _dot_nt(a, b):
    return lax.dot_general(a, b, (((1,), (1,)), ((), ())), preferred_element_type=F32)


def _dot_tn(a, b):
    return lax.dot_general(a, b, (((0,), (0,)), ((), ())), preferred_element_type=F32)


def _ln_kernel(x_ref, g_ref, b_ref, o32_ref, o16_ref):
    y = _layer_norm_rows(x_ref[...], g_ref[...], b_ref[...])
    o32_ref[...] = y
    o16_ref[...] = y.astype(BF16)


def _ln(x, g, b, bm=512):
    s, d = x.shape
    row = pl.BlockSpec((bm, d), lambda i: (i, 0))
    vec = pl.BlockSpec((1, d), lambda i: (0, 0))
    return pl.pallas_call(
        _ln_kernel,
        grid=(s // bm,),
        in_specs=[row, vec, vec],
        out_specs=[row, row],
        out_shape=[jax.ShapeDtypeStruct((s, d), F32), jax.ShapeDtypeStruct((s, d), BF16)],
        compiler_params=_params(("parallel",)),
        name="ln_in",
    )(x, g.reshape(1, d), b.reshape(1, d))


def _in_proj_kernel(x_ref, w_ref, cos_ref, sin_ref, gb_ref, wo_ref, wf_ref,
                    o_ref, wo16_ref, wf16_ref, w16_ref):
    col_block = pl.program_id(0)
    bm, bn = o_ref.shape
    assert bn == RET_HEADS * RET_HEAD_DIM
    rg = min(bm, IN_PROJ_ROW_GROUP)
    half = RET_HEAD_DIM // 2

    @pl.when(pl.program_id(1) == 0)
    def _():
        w16_ref[...] = w_ref[...].astype(BF16)

    def tile(epilogue):
        for r in range(bm // rg):
            rows = slice(r * rg, (r + 1) * rg)
            acc = _dot(x_ref[rows, :], w16_ref[...])
            if r == 0:
                wo16_ref[...] = wo_ref[...].astype(BF16)
                wf16_ref[...] = wf_ref[...].astype(BF16)
            if epilogue == "rope":
                c = cos_ref[rows, :]
                s = sin_ref[rows, :]
                scale = jnp.where(col_block == COL_RK // bn, RET_HEAD_DIM ** -0.5, 1.0)
                for h in range(RET_HEADS):
                    lo = slice(h * RET_HEAD_DIM, h * RET_HEAD_DIM + half)
                    hi = slice(h * RET_HEAD_DIM + half, (h + 1) * RET_HEAD_DIM)
                    t1, t2 = acc[:, lo], acc[:, hi]
                    o_ref[rows, lo] = ((t1 * c - t2 * s) * scale).astype(BF16)
                    o_ref[rows, hi] = ((t1 * s + t2 * c) * scale).astype(BF16)
            elif epilogue == "gate":
                o_ref[rows, :] = _sigmoid(acc + gb_ref[...]).astype(BF16)
            else:
                scale = jnp.where(col_block == COL_NQ // bn, NA_Q_SCALE, 1.0)
                o_ref[rows, :] = (acc * scale).astype(BF16)

    pl.when(col_block < COL_RV // bn)(functools.partial(tile, "rope"))
    pl.when(jnp.logical_and(col_block >= COL_RV // bn, col_block < COL_GATE // bn))(
        functools.partial(tile, "mid"))
    pl.when(col_block >= COL_GATE // bn)(functools.partial(tile, "gate"))


def _cast_chunks(rows, steps):
    n = max(c for c in range(1, steps + 1) if rows % c == 0 and (rows // c) % 128 == 0)
    return n, rows // n


def _in_proj(x16, w, layer, cos, sin, gate_b3, w_out, w_ffn_out, bm=2048):
    s, k = x16.shape
    bm = min(bm, s)
    bn = IN_PROJ_BN
    nrow = s // bm
    first_gate = COL_GATE // bn
    tab = pl.BlockSpec((bm, RET_HEAD_DIM // 2), lambda j, i: (i, 0))
    steps = (IN_COLS // bn) * nrow
    d = D_MODEL

    def chunked(total_rows):
        n, rows = _cast_chunks(total_rows, steps)
        chunk = lambda j, i: jnp.minimum(j * nrow + i, n - 1)
        return (pl.BlockSpec((None, rows, d), lambda j, i: (layer, chunk(j, i), 0)),
                pl.BlockSpec((rows, d), lambda j, i: (chunk(j, i), 0)))

    wo_in, wo_out = chunked(w_out.shape[1])
    wf_in, wf_out = chunked(w_ffn_out.shape[1])
    return pl.pallas_call(
        _in_proj_kernel,
        grid=(IN_COLS // bn, nrow),
        in_specs=[pl.BlockSpec((bm, k), lambda j, i: (i, 0)),
                  pl.BlockSpec((None, k, bn), lambda j, i: (layer, 0, j)),
                  tab, tab,
                  pl.BlockSpec((None, 1, bn), lambda j, i: (layer, 0, jnp.maximum(j - first_gate, 0))),
                  wo_in, wf_in],
        out_specs=[pl.BlockSpec((bm, bn), lambda j, i: (i, j)), wo_out, wf_out],
        out_shape=[jax.ShapeDtypeStruct((s, IN_COLS), BF16),
                   jax.ShapeDtypeStruct(w_out.shape[1:], BF16),
                   jax.ShapeDtypeStruct(w_ffn_out.shape[1:], BF16)],
        scratch_shapes=[pltpu.VMEM((k, bn), BF16)],
        compiler_params=_params(("arbitrary", "arbitrary")),
        name="in_proj",
    )(x16, w, cos, sin, gate_b3, w_out, w_ffn_out)


def _chunk_pos():
    return lax.broadcasted_iota(jnp.int32, (RET_BLOCK, RET_HEAD_DIM), 0).astype(F32)


def _head_cols(h):
    return slice(h * RET_HEAD_DIM, (h + 1) * RET_HEAD_DIM)


def _ret_kernel(q_ref, k_ref, v_ref, g_ref, dl_ref, o_ref,
                sb_ref, sf_ref, qdb_ref, kdb_ref, qdf_ref, kdf_ref, m_ref, ib_ref, *, nc):
    c = RET_BLOCK
    assert c <= RET_HEAD_DIM
    phase = pl.program_id(0)
    n = pl.program_id(1)

    @pl.when(jnp.logical_and(phase == 0, n == 0))
    def _():
        sb_ref[...] = jnp.zeros_like(sb_ref)
        sf_ref[...] = jnp.zeros_like(sf_ref)
        pos = _chunk_pos()
        ri = lax.broadcasted_iota(jnp.int32, (c, c), 0)
        ci = lax.broadcasted_iota(jnp.int32, (c, c), 1)
        diff = (ri - ci).astype(F32)
        for h in range(RET_HEADS):
            lgf = _log_sigmoid(dl_ref[h])
            lgb = _log_sigmoid(dl_ref[RET_HEADS + h])
            qdb_ref[h] = jnp.exp((c - pos) * lgb)
            kdb_ref[h] = jnp.exp(pos * lgb)
            qdf_ref[h] = jnp.exp((pos + 1.0) * lgf)
            kdf_ref[h] = jnp.exp((c - 1.0 - pos) * lgf)
            m_ref[h] = jnp.exp(jnp.abs(diff) * jnp.where(diff >= 0, lgf[:, :c], lgb[:, :c]))

    @pl.when(phase == 0)
    def _():
        rows = pl.ds(pl.multiple_of((nc - 1 - n) * c, c), c)
        for h in range(RET_HEADS):
            cols = _head_cols(h)
            c_dec = jnp.exp(c * _log_sigmoid(dl_ref[RET_HEADS + h]))
            state = sb_ref[h]
            ib_ref[rows, cols] = _dot(q_ref[:, cols], state.astype(BF16)) * qdb_ref[h]
            vk = (v_ref[:, cols].astype(F32) * kdb_ref[h]).astype(BF16)
            sb_ref[h] = c_dec * state + _dot_tn(k_ref[:, cols], vk)

    @pl.when(phase == 1)
    def _():
        rows = pl.ds(pl.multiple_of(n * c, c), c)
        for h in range(RET_HEADS):
            cols = _head_cols(h)
            q = q_ref[:, cols]
            k = k_ref[:, cols]
            v = v_ref[:, cols]
            p = (_dot_nt(q, k) * m_ref[h]).astype(BF16)
            intra = _dot(p, v)
            state = sf_ref[h]
            inter = _dot(q, state.astype(BF16)) * qdf_ref[h]
            y = intra + inter + ib_ref[rows, cols]

            c_dec = jnp.exp(c * _log_sigmoid(dl_ref[h]))
            vk = (v.astype(F32) * kdf_ref[h]).astype(BF16)
            sf_ref[h] = c_dec * state + _dot_tn(k, vk)

            mu = jnp.mean(y, axis=-1, keepdims=True)
            yc = y - mu
            var = jnp.mean(yc * yc, axis=-1, keepdims=True)
            yn = yc * lax.rsqrt(var + LN_EPS)
            g = g_ref[:, cols].astype(F32)
            o_ref[:, cols] = (g * _sigmoid(g) * yn).astype(BF16)


def _retention(proj16, decay_lanes):
    s = proj16.shape[0]
    c, d, nh = RET_BLOCK, RET_HEAD_DIM, RET_HEADS
    nc = s // c
    w = nh * d
    table = pltpu.VMEM((nh, c, d), F32)

    def chunk(col):
        return pl.BlockSpec((c, w), lambda p, n: (jnp.where(p == 0, nc - 1 - n, n), col))

    def fwd_only(col):
        return pl.BlockSpec((c, w), lambda p, n: (jnp.where(p == 0, 0, n), col))

    return pl.pallas_call(
        functools.partial(_ret_kernel, nc=nc),
        grid=(2, nc),
        in_specs=[chunk(COL_RQ // w), chunk(COL_RK // w), chunk(COL_RV // w), fwd_only(COL_RG // w),
                  pl.BlockSpec((2 * nh, 1, d), lambda p, n: (0, 0, 0))],
        out_specs=fwd_only(0),
        out_shape=jax.ShapeDtypeStruct((s, w), BF16),
        scratch_shapes=[pltpu.VMEM((nh, d, d), F32), pltpu.VMEM((nh, d, d), F32),
                        table, table, table, table, pltpu.VMEM((nh, c, c), F32),
                        pltpu.VMEM((s, w), F32)],
        compiler_params=_params(("arbitrary", "arbitrary")),
        name="retention",
    )(proj16, proj16, proj16, proj16, decay_lanes)


_NA_SPAN = NA_KEY_ROWS - NA_KH
_NA_START_PAIR = ((0, 0), (0, 2), (2, 2))


def _na_first_key_row(case, i):
    return (max(i - NA_KH // 2, 0), i, min(i + NA_KH // 2, _NA_SPAN))[case]


def _na_build_bias(rpb_ref, bias_ref, head):
    w = GRID_W
    n_dc = 2 * NA_KW - 1
    n_dr = 2 * NA_KH - 1
    lane = lax.broadcasted_iota(jnp.int32, (w, 2 * w), 1)
    cq = lax.broadcasted_iota(jnp.int32, (w, 2 * w), 0)
    ck = lane & (w - 1)
    dc = jnp.clip(ck - cq, -(NA_KW - 1), NA_KW - 1) + (NA_KW - 1)
    col_start = jnp.clip(cq - NA_KW // 2, 0, w - NA_KW)
    in_win = (ck >= col_start) & (ck < col_start + NA_KW)
    dc_is = [dc == e for e in range(n_dc)]
    neg = jnp.full((w, 2 * w), NA_NEG, F32)
    rows = []
    for dr in range(n_dr):
        t = jnp.zeros((w, 2 * w), F32)
        base = (head * n_dr + dr) * n_dc
        for e in range(n_dc):
            t = jnp.where(dc_is[e], rpb_ref[base + e], t)
        rows.append(jnp.where(in_win, t * LOG2_E, neg))
    left = lane < w
    for case, off in enumerate((0, _NA_SPAN // 2, _NA_SPAN)):
        for u in range(NA_ROWS_PER_STEP // NA_UNIT_ROWS):
            sp = _NA_START_PAIR[case][u]
            for ii in range(NA_UNIT_ROWS):
                i = u * NA_UNIT_ROWS + ii
                first = _na_first_key_row(case, i)
                assert 2 * sp <= first and first + NA_KH <= 2 * (sp + NA_UNIT_PAIRS)

                def tile(j):
                    if first <= j < first + NA_KH:
                        return rows[j - i + (NA_KH - 1) - off]
                    return None

                for jj in range(NA_UNIT_PAIRS):
                    a, b = tile(2 * (sp + jj)), tile(2 * (sp + jj) + 1)
                    if a is None and b is None:
                        blk = neg
                    else:
                        blk = jnp.where(left, neg if a is None else a, neg if b is None else b)
                    bias_ref[case, u, ii * w:(ii + 1) * w, jj * 2 * w:(jj + 1) * 2 * w] = blk


def _na_kernel(rpb_ref, q_ref, k_ref, v_ref, o_ref, bias_ref, *, rows):
    rb = pl.program_id(1)
    nrb = rows // NA_ROWS_PER_STEP
    d = NA_HEAD_DIM
    uq = NA_UNIT_ROWS * GRID_W
    uk = NA_UNIT_PAIRS * 2 * GRID_W

    @pl.when(rb == 0)
    def _():
        for hh in range(NA_HEADS_PER_STEP):
            _na_build_bias(rpb_ref, bias_ref.at[hh], pl.program_id(0) * NA_HEADS_PER_STEP + hh)

    key_row0 = jnp.clip(rb * NA_ROWS_PER_STEP - NA_KH // 2, 0, rows - NA_KEY_ROWS)
    case = jnp.where(rb == 0, 0, jnp.where(rb == nrb - 1, 2, 1))
    for u in range(NA_ROWS_PER_STEP // NA_UNIT_ROWS):
        first, mid, last = (_NA_START_PAIR[cs][u] for cs in range(3))
        sp = jnp.where(rb == 0, first, jnp.where(rb == nrb - 1, last, mid))
        kstart = pl.multiple_of((key_row0 + 2 * sp) * GRID_W, 2 * GRID_W)
        for hh in range(NA_HEADS_PER_STEP):
            cols = slice(hh * d, (hh + 1) * d)
            kk = k_ref[pl.ds(kstart, uk), cols]
            vv = v_ref[pl.ds(kstart, uk), cols]
            q = q_ref[u * uq:(u + 1) * uq, cols]
            s = _dot_nt(q, kk) + bias_ref[hh, case, u]
            m = jnp.max(s, axis=-1, keepdims=True)
            e = jnp.exp2(s - m)
            l = jnp.sum(e, axis=-1, keepdims=True)
            o = _dot(e.astype(BF16), vv) / l
            o_ref[u * uq:(u + 1) * uq, cols] = o.astype(BF16)


def _neighborhood_attention(proj16, rpb_flat):
    s = proj16.shape[0]
    rows = s // GRID_W
    d = NA_HEAD_DIM * NA_HEADS_PER_STEP
    bq = NA_ROWS_PER_STEP * GRID_W
    nrb = rows // NA_ROWS_PER_STEP
    n_units = NA_ROWS_PER_STEP // NA_UNIT_ROWS
    return pl.pallas_call(
        functools.partial(_na_kernel, rows=rows),
        grid=(NA_HEADS // NA_HEADS_PER_STEP, nrb),
        in_specs=[pl.BlockSpec(memory_space=pltpu.SMEM),
                  pl.BlockSpec((bq, d), lambda h, r: (r, COL_NQ // d + h)),
                  pl.BlockSpec((s, d), lambda h, r: (0, COL_NK // d + h)),
                  pl.BlockSpec((s, d), lambda h, r: (0, COL_NV // d + h))],
        out_specs=pl.BlockSpec((bq, d), lambda h, r: (r, h)),
        out_shape=jax.ShapeDtypeStruct((s, NA_HEADS * NA_HEAD_DIM), BF16),
        scratch_shapes=[pltpu.VMEM((NA_HEADS_PER_STEP, 3, n_units, NA_UNIT_ROWS * GRID_W,
                                    NA_UNIT_PAIRS * 2 * GRID_W), F32)],
        compiler_params=_params(("parallel", "arbitrary")),
        name="natten",
    )(rpb_flat, proj16, proj16, proj16)


def _gelu_tanh(x):
    return x * (0.5 * (1.0 + jnp.tanh(math.sqrt(2.0 / math.pi) * (x + 0.044715 * (x * x * x)))))


def _lru_kernel(x_ref, y_ref, wc_ref, bc_ref, waf_ref, wif_ref, wab_ref, wib_ref, ba_ref, bi_ref,
                lam_ref, o_ref, xpad_ref, xc_ref, hf_ref, a_ref, b_ref, hb_ref, *, seq):
    tt = min(LRU_TILE, seq)
    nt = seq // tt
    ng = tt // 8
    lanes = LRU_BLOCK_DIM
    row = lax.broadcasted_iota(jnp.int32, (8, lanes), 0)

    zeros_halo = jnp.zeros((LRU_HALO, lanes), F32)
    xpad_ref[0:LRU_HALO, :] = zeros_halo
    xpad_ref[LRU_HALO + seq:2 * LRU_HALO + seq, :] = zeros_halo

    def stage(t, carry):
        t0 = pl.multiple_of(t * tt, tt)
        xpad_ref[pl.ds(LRU_HALO + t0, tt), :] = x_ref[pl.ds(t0, tt), :].astype(F32)
        return carry

    lax.fori_loop(0, nt, stage, 0)

    def conv_tile(t0):
        base = LRU_HALO - LRU_CONV // 2
        xc = bc_ref[...]
        for j in range(LRU_CONV):
            xc = xc + xpad_ref[pl.ds(t0 + base + j, tt), :] * wc_ref[j:j + 1, :]
        return xc

    def tile_gates(xc, wa_ref, wi_ref, d):
        xb = xc.astype(BF16)
        t_r = jnp.tanh(_dot(xb, (0.5 * wa_ref[...]).astype(BF16)) + 0.5 * ba_ref[d:d + 1, :])
        t_i = jnp.tanh(_dot(xb, (0.5 * wi_ref[...]).astype(BF16)) + 0.5 * bi_ref[d:d + 1, :])
        neg_log_a = (0.5 * LRU_C * _softplus(-lam_ref[d:d + 1, :])) * (t_r + 1.0)
        a = jnp.exp(-neg_log_a)
        a_ref[...] = a
        gain = _sqrt_nonneg(jnp.tanh(neg_log_a) * (a * a + 1.0))
        b_ref[...] = gain * ((0.5 * t_i + 0.5) * xc)

    def fwd_tile(t, carry):
        t0 = pl.multiple_of(t * tt, tt)
        xc = conv_tile(t0)
        xc_ref[pl.ds(t0, tt), :] = xc
        tile_gates(xc, waf_ref, wif_ref, 0)

        def group(gi, c):
            r0 = pl.multiple_of(gi * 8, 8)
            a = a_ref[pl.ds(r0, 8), :]
            b = b_ref[pl.ds(r0, 8), :]
            for sh in (1, 2, 4):
                keep = row >= sh
                a_sh = jnp.where(keep, pltpu.roll(a, sh, 0), 1.0)
                b_sh = jnp.where(keep, pltpu.roll(b, sh, 0), 0.0)
                b = a * b_sh + b
                a = a * a_sh
            h = a * c + b
            hf_ref[pl.ds(t0 + r0, 8), :] = h
            return jnp.broadcast_to(h[7:8, :], (8, lanes))

        return lax.fori_loop(0, ng, group, carry, unroll=8)

    lax.fori_loop(0, nt, fwd_tile, jnp.zeros((8, lanes), F32))

    def bwd_tile(ti, carry):
        t0 = pl.multiple_of((nt - 1 - ti) * tt, tt)
        tile_gates(xc_ref[pl.ds(t0, tt), :], wab_ref, wib_ref, 1)

        def group(gj, c):
            r0 = pl.multiple_of((ng - 1 - gj) * 8, 8)
            a = a_ref[pl.ds(r0, 8), :]
            b = b_ref[pl.ds(r0, 8), :]
            for sh in (1, 2, 4):
                keep = row < 8 - sh
                a_sh = jnp.where(keep, pltpu.roll(a, 8 - sh, 0), 1.0)
                b_sh = jnp.where(keep, pltpu.roll(b, 8 - sh, 0), 0.0)
                b = a * b_sh + b
                a = a * a_sh
            h = a * c + b
            hb_ref[pl.ds(r0, 8), :] = h
            return jnp.broadcast_to(h[0:1, :], (8, lanes))

        carry = lax.fori_loop(0, ng, group, carry, unroll=8)
        h = hf_ref[pl.ds(t0, tt), :] + hb_ref[...]
        y = y_ref[pl.ds(t0, tt), :].astype(F32)
        o_ref[pl.ds(t0, tt), :] = (h * _gelu_tanh(y)).astype(BF16)
        return carry

    lax.fori_loop(0, nt, bwd_tile, jnp.zeros((8, lanes), F32))


def _rglru(proj16, w_conv, b_conv, wa, ba, wi, bi, lam):
    s = proj16.shape[0]
    bd = LRU_BLOCK_DIM
    tt = min(LRU_TILE, s)
    col = lambda off: pl.BlockSpec((s, bd), lambda g: (0, off // bd + g))
    gate_w = lambda d: pl.BlockSpec((None, None, bd, bd), lambda g: (d, g, 0, 0))
    vec2 = pl.BlockSpec((2, bd), lambda g: (0, g))
    return pl.pallas_call(
        functools.partial(_lru_kernel, seq=s),
        grid=(LRU_BLOCKS,),
        in_specs=[col(COL_LX), col(COL_LY),
                  pl.BlockSpec((LRU_CONV, bd), lambda g: (0, g)),
                  pl.BlockSpec((1, bd), lambda g: (0, g)),
                  gate_w(0), gate_w(0), gate_w(1), gate_w(1),
                  vec2, vec2, vec2],
        out_specs=pl.BlockSpec((s, bd), lambda g: (0, g)),
        out_shape=jax.ShapeDtypeStruct((s, LRU_WIDTH), BF16),
        scratch_shapes=[pltpu.VMEM((s + 2 * LRU_HALO, bd), F32),
                        pltpu.VMEM((s, bd), F32),
                        pltpu.VMEM((s, bd), F32),
                        pltpu.VMEM((tt, bd), F32),
                        pltpu.VMEM((tt, bd), F32),
                        pltpu.VMEM((tt, bd), F32)],
        compiler_params=_params(("parallel",)),
        name="rglru",
    )(proj16, proj16, w_conv, b_conv, wa, wi, wa, wi, ba, bi, lam)


def _merge_kernel(r_ref, n_ref, l_ref, wr_ref, wn_ref, wl_ref, gr_ref, gn_ref, gl_ref, o_ref,
                  wr16_ref, wn16_ref, wl16_ref):
    @pl.when(pl.program_id(1) == 0)
    def _():
        for w_ref, w16_ref in ((wr_ref, wr16_ref), (wn_ref, wn16_ref), (wl_ref, wl16_ref)):
            w16_ref[...] = w_ref[...].astype(BF16)

    acc = None
    for x_ref, w16_ref, g_ref in ((r_ref, wr16_ref, gr_ref), (n_ref, wn16_ref, gn_ref),
                                  (l_ref, wl16_ref, gl_ref)):
        term = g_ref[...].astype(F32) * _dot(x_ref[...], w16_ref[...])
        acc = term if acc is None else acc + term
    o_ref[...] = acc.astype(BF16)


def _merge(ret16, na16, lru16, proj16, w_branch, layer, bm=512, bn=1024):
    s = ret16.shape[0]
    d = D_MODEL
    kw = BRANCH_WIDTH
    xin = pl.BlockSpec((bm, kw), lambda j, i: (i, 0))
    wsp = lambda b: pl.BlockSpec((None, None, kw, bn), lambda j, i: (layer, b, 0, j))
    gsp = lambda b: pl.BlockSpec((bm, bn), lambda j, i: (i, (COL_GATE + b * d) // bn + j))
    return pl.pallas_call(
        _merge_kernel,
        grid=(d // bn, s // bm),
        in_specs=[xin, xin, xin, wsp(0), wsp(1), wsp(2), gsp(0), gsp(1), gsp(2)],
        out_specs=pl.BlockSpec((bm, bn), lambda j, i: (i, j)),
        out_shape=jax.ShapeDtypeStruct((s, d), BF16),
        scratch_shapes=[pltpu.VMEM((kw, bn), BF16)] * N_BRANCH,
        compiler_params=_params(("parallel", "arbitrary")),
        name="merge",
    )(ret16, na16, lru16, w_branch, w_branch, w_branch, proj16, proj16, proj16)


def _mm_res_kernel(x_ref, w_ref, h_ref, o_ref):
    o_ref[...] = DEEPNORM_ALPHA * h_ref[...] + _dot(x_ref[...], w_ref[...])


def _mm_res_ln_kernel(x_ref, w_ref, r_ref, g_ref, b_ref, o32_ref, o16_ref, *, res_scale):
    bm = o32_ref.shape[0]
    rg = bm // LN_ROW_GROUPS
    for rows in (slice(r * rg, (r + 1) * rg) for r in range(LN_ROW_GROUPS)):
        res = r_ref[rows, :] if res_scale == 1.0 else res_scale * r_ref[rows, :]
        y = _layer_norm_rows(res + _dot(x_ref[rows, :], w_ref[...]), g_ref[...], b_ref[...])
        o32_ref[rows, :] = y
        o16_ref[rows, :] = y.astype(BF16)


def _kblock_specs(bm, bk, kblock):
    return [pl.BlockSpec((bm, bk), lambda i: (i, kblock)),
            pl.BlockSpec((bk, D_MODEL), lambda i: (kblock, 0))]


def _mm_res(x16, w16, h32, name, bm, bk, kblock):
    s = x16.shape[0]
    row = pl.BlockSpec((bm, D_MODEL), lambda i: (i, 0))
    return pl.pallas_call(
        _mm_res_kernel,
        grid=(s // bm,),
        in_specs=_kblock_specs(bm, bk, kblock) + [row],
        out_specs=row,
        out_shape=jax.ShapeDtypeStruct((s, D_MODEL), F32),
        compiler_params=_params(("parallel",)),
        name=name,
    )(x16, w16, h32)


def _mm_res_ln(x16, w16, layer, res32, res_scale, g, b, name, bm, bk, kblock=0):
    s = x16.shape[0]
    d = D_MODEL
    row = pl.BlockSpec((bm, d), lambda i: (i, 0))
    vec = pl.BlockSpec((None, 1, d), lambda i: (layer, 0, 0))
    return pl.pallas_call(
        functools.partial(_mm_res_ln_kernel, res_scale=res_scale),
        grid=(s // bm,),
        in_specs=_kblock_specs(bm, bk, kblock) + [row, vec, vec],
        out_specs=[row, row],
        out_shape=[jax.ShapeDtypeStruct((s, d), F32), jax.ShapeDtypeStruct((s, d), BF16)],
        compiler_params=_params(("parallel",)),
        name=name,
    )(x16, w16, res32, g, b)


def _ffn_in_kernel(x_ref, wg_ref, wv_ref, o_ref, wg16_ref, wv16_ref):
    @pl.when(pl.program_id(1) == 0)
    def _():
        wg16_ref[...] = wg_ref[...].astype(BF16)
        wv16_ref[...] = wv_ref[...].astype(BF16)

    bm = o_ref.shape[0]
    rg = min(bm, IN_PROJ_ROW_GROUP)
    for r in range(bm // rg):
        rows = slice(r * rg, (r + 1) * rg)
        x = x_ref[rows, :]
        gate = _dot(x, wg16_ref[...])
        val = _dot(x, wv16_ref[...])
        o_ref[rows, :] = (gate * _sigmoid(gate) * val).astype(BF16)


def _ffn_in(h16, w, layer, bm=2048, bn=512):
    s, k = h16.shape
    bm = min(bm, s)
    nb = D_FF // bn
    return pl.pallas_call(
        _ffn_in_kernel,
        grid=(nb, s // bm),
        in_specs=[pl.BlockSpec((bm, k), lambda j, i: (i, 0)),
                  pl.BlockSpec((None, k, bn), lambda j, i: (layer, 0, j)),
                  pl.BlockSpec((None, k, bn), lambda j, i: (layer, 0, nb + j))],
        out_specs=pl.BlockSpec((bm, bn), lambda j, i: (i, j)),
        out_shape=jax.ShapeDtypeStruct((s, D_FF), BF16),
        scratch_shapes=[pltpu.VMEM((k, bn), BF16), pltpu.VMEM((k, bn), BF16)],
        compiler_params=_params(("parallel", "arbitrary")),
        name="ffn_in",
    )(h16, w, w)


def _rotary_tables(seq):
    half = RET_HEAD_DIM // 2
    inv = ROPE_BASE ** (-jnp.arange(half, dtype=F32) / half)
    ang = jnp.arange(seq, dtype=F32)[:, None] * inv[None, :]
    return jnp.cos(ang), jnp.sin(ang)


def _mixer_branches(proj16, ret_decay, w_conv, b_conv, lru_wa, lru_ba, lru_wi, lru_bi,
                    lru_lambda, na_rpb):
    decay_lanes = jnp.broadcast_to(ret_decay.reshape(2 * RET_HEADS, 1, 1),
                                   (2 * RET_HEADS, 1, RET_HEAD_DIM))
    ret16 = _retention(proj16, decay_lanes)
    na16 = _neighborhood_attention(proj16, na_rpb.reshape(-1))
    lru16 = _rglru(proj16, w_conv, b_conv.reshape(1, LRU_WIDTH), lru_wa, lru_ba, lru_wi, lru_bi,
                   lru_lambda)
    return ret16, na16, lru16


def kernel(x, ln_in_g, ln_in_b, w_in, gate_b, ret_decay, w_conv, b_conv, lru_wa, lru_ba, lru_wi, lru_bi, lru_lambda, na_rpb, w_branch, w_out, ln1_g, ln1_b, w_ffn_in, w_ffn_out, ln2_g, ln2_b):
    batch, seq, d = x.shape
    assert batch == 1 and d == D_MODEL
    gate_b3 = gate_b.reshape(DEPTH, 1, N_BRANCH * d)
    vec3 = lambda v: v.reshape(DEPTH, 1, d)
    ln1_g3, ln1_b3, ln2_g3, ln2_b3 = vec3(ln1_g), vec3(ln1_b), vec3(ln2_g), vec3(ln2_b)
    cos, sin = _rotary_tables(seq)

    h32, h16 = _ln(x.reshape(seq, d), ln_in_g, ln_in_b)
    for l in range(DEPTH):
        proj16, w_out16, w_ffn_out16 = _in_proj(h16, w_in, l, cos, sin, gate_b3, w_out, w_ffn_out)
        ret16, na16, lru16 = _mixer_branches(proj16, ret_decay[l], w_conv[l], b_conv[l],
                                             lru_wa[l], lru_ba[l], lru_wi[l], lru_bi[l],
                                             lru_lambda[l], na_rpb[l])
        merged16 = _merge(ret16, na16, lru16, proj16, w_branch, l)
        h32, h16 = _mm_res_ln(merged16, w_out16, l, h32, DEEPNORM_ALPHA, ln1_g3, ln1_b3, "out_proj",
                              bm=512, bk=D_MODEL)
        act16 = _ffn_in(h16, w_ffn_in, l)
        part = _mm_res(act16, w_ffn_out16, h32, "ffn_out_a", bm=512, bk=D_FF // 2, kblock=0)
        h32, h16 = _mm_res_ln(act16, w_ffn_out16, l, part, 1.0, ln2_g3, ln2_b3, "ffn_out_b",
                              bm=512, bk=D_FF // 2, kblock=1)
    return h32.reshape(batch, seq, d)
```

```python
import functools
import math

import jax
import jax.numpy as jnp
from jax import lax
from jax.experimental import pallas as pl
from jax.experimental.pallas import tpu as pltpu

F32 = jnp.float32
BF16 = jnp.bfloat16

D_MODEL = 2048
DEPTH = 4
GRID_W = 64
RET_HEADS = 4
RET_HEAD_DIM = 256
NA_HEADS = 8
NA_HEAD_DIM = 128
NA_KH = 8
NA_KW = 16
LRU_WIDTH = 1024
LRU_BLOCKS = 8
LRU_BLOCK_DIM = 128
LRU_CONV = 4
LRU_C = 8.0
N_BRANCH = 3
BRANCH_WIDTH = 1024
IN_COLS = 15360
D_FF = 5632
DEEPNORM_ALPHA = (2 * DEPTH) ** 0.25
LN_EPS = 1e-5
ROPE_BASE = 10000.0

COL_RQ, COL_RK, COL_RV, COL_RG = 0, 1024, 2048, 3072
COL_NQ, COL_NK, COL_NV = 4096, 5120, 6144
COL_LX, COL_LY, COL_GATE = 7168, 8192, 9216
IN_PROJ_BN = 1024
IN_PROJ_ROW_GROUP = 256

RET_BLOCK = 256

NA_ROWS_PER_STEP = 8
NA_KEY_ROWS = 16
NA_UNIT_ROWS = 4
NA_UNIT_PAIRS = 6
NA_HEADS_PER_STEP = 2
LOG2_E = math.log2(math.e)
NA_Q_SCALE = NA_HEAD_DIM ** -0.5 * LOG2_E
NA_NEG = -1e30

LRU_TILE = 2048
LRU_HALO = 8

LN_ROW_GROUPS = 4

VMEM_LIMIT = 56 * 1024 * 1024


def _params(sem, vmem=VMEM_LIMIT):
    return pltpu.CompilerParams(dimension_semantics=sem, vmem_limit_bytes=vmem)


def _softplus(x):
    return jnp.maximum(x, 0.0) + jnp.log1p(jnp.exp(-jnp.abs(x)))


def _log_sigmoid(x):
    return -_softplus(-x)


def _sigmoid(x):
    return 0.5 * jnp.tanh(0.5 * x) + 0.5


def _sqrt_nonneg(x):
    return jnp.where(x > 0.0, x * lax.rsqrt(x), 0.0)


def _layer_norm_rows(x, g, b):
    mu = jnp.mean(x, axis=-1, keepdims=True)
    xc = x - mu
    var = jnp.mean(xc * xc, axis=-1, keepdims=True)
    return xc * lax.rsqrt(var + LN_EPS) * g + b


def _dot(a, b):
    return jnp.dot(a, b, preferred_element_type=F32)


def _dot_nt(a, b):
    return lax.dot_general(a, b, (((1,), (1,)), ((), ())), preferred_element_type=F32)


def _dot_tn(a, b):
    return lax.dot_general(a, b, (((0,), (0,)), ((), ())), preferred_element_type=F32)


def _ln_kernel(x_ref, g_ref, b_ref, o32_ref, o16_ref):
    y = _layer_norm_rows(x_ref[...], g_ref[...], b_ref[...])
    o32_ref[...] = y
    o16_ref[...] = y.astype(BF16)


def _ln(x, g, b, bm=512):
    s, d = x.shape
    row = pl.BlockSpec((bm, d), lambda i: (i, 0))
    vec = pl.BlockSpec((1, d), lambda i: (0, 0))
    return pl.pallas_call(
        _ln_kernel,
        grid=(s // bm,),
        in_specs=[row, vec, vec],
        out_specs=[row, row],
        out_shape=[jax.ShapeDtypeStruct((s, d), F32), jax.ShapeDtypeStruct((s, d), BF16)],
        compiler_params=_params(("parallel",)),
        name="ln_in",
    )(x, g.reshape(1, d), b.reshape(1, d))


def _in_proj_kernel(x_ref, w_ref, cos_ref, sin_ref, gb_ref, wo_ref, wf_ref,
                    o_ref, wo16_ref, wf16_ref, w16_ref):
    col_block = pl.program_id(0)
    bm, bn = o_ref.shape
    assert bn == RET_HEADS * RET_HEAD_DIM
    rg = min(bm, IN_PROJ_ROW_GROUP)
    half = RET_HEAD_DIM // 2

    @pl.when(pl.program_id(1) == 0)
    def _():
        halve = jnp.where(col_block >= COL_GATE // bn, 0.5, 1.0)
        w16_ref[...] = (w_ref[...] * halve).astype(BF16)

    def tile(epilogue):
        for r in range(bm // rg):
            rows = slice(r * rg, (r + 1) * rg)
            acc = _dot(x_ref[rows, :], w16_ref[...])
            if r == 0:
                wo16_ref[...] = wo_ref[...].astype(BF16)
                wf16_ref[...] = wf_ref[...].astype(BF16)
            if epilogue == "rope":
                scale = jnp.where(col_block == COL_RK // bn, RET_HEAD_DIM ** -0.5, 1.0)
                c = cos_ref[rows, :] * scale
                s = sin_ref[rows, :] * scale
                for h in range(RET_HEADS):
                    lo = slice(h * RET_HEAD_DIM, h * RET_HEAD_DIM + half)
                    hi = slice(h * RET_HEAD_DIM + half, (h + 1) * RET_HEAD_DIM)
                    t1, t2 = acc[:, lo], acc[:, hi]
                    o_ref[rows, lo] = (t1 * c - t2 * s).astype(BF16)
                    o_ref[rows, hi] = (t1 * s + t2 * c).astype(BF16)
            elif epilogue == "gate":
                o_ref[rows, :] = (0.5 * jnp.tanh(acc + 0.5 * gb_ref[...]) + 0.5).astype(BF16)
            else:
                scale = jnp.where(col_block == COL_NQ // bn, NA_Q_SCALE, 1.0)
                o_ref[rows, :] = (acc * scale).astype(BF16)

    pl.when(col_block < COL_RV // bn)(functools.partial(tile, "rope"))
    pl.when(jnp.logical_and(col_block >= COL_RV // bn, col_block < COL_GATE // bn))(
        functools.partial(tile, "mid"))
    pl.when(col_block >= COL_GATE // bn)(functools.partial(tile, "gate"))


def _cast_chunks(rows, steps):
    n = max(c for c in range(1, steps + 1) if rows % c == 0 and (rows // c) % 128 == 0)
    return n, rows // n


def _in_proj(x16, w, layer, cos, sin, gate_b3, w_out, w_ffn_out, bm=2048):
    s, k = x16.shape
    bm = min(bm, s)
    bn = IN_PROJ_BN
    nrow = s // bm
    first_gate = COL_GATE // bn
    tab = pl.BlockSpec((bm, RET_HEAD_DIM // 2), lambda j, i: (i, 0))
    steps = (IN_COLS // bn) * nrow
    d = D_MODEL

    def chunked(total_rows):
        n, rows = _cast_chunks(total_rows, steps)
        chunk = lambda j, i: jnp.minimum(j * nrow + i, n - 1)
        return (pl.BlockSpec((None, rows, d), lambda j, i: (layer, chunk(j, i), 0)),
                pl.BlockSpec((rows, d), lambda j, i: (chunk(j, i), 0)))

    wo_in, wo_out = chunked(w_out.shape[1])
    wf_in, wf_out = chunked(w_ffn_out.shape[1])
    return pl.pallas_call(
        _in_proj_kernel,
        grid=(IN_COLS // bn, nrow),
        in_specs=[pl.BlockSpec((bm, k), lambda j, i: (i, 0)),
                  pl.BlockSpec((None, k, bn), lambda j, i: (layer, 0, j)),
                  tab, tab,
                  pl.BlockSpec((None, 1, bn), lambda j, i: (layer, 0, jnp.maximum(j - first_gate, 0))),
                  wo_in, wf_in],
        out_specs=[pl.BlockSpec((bm, bn), lambda j, i: (i, j)), wo_out, wf_out],
        out_shape=[jax.ShapeDtypeStruct((s, IN_COLS), BF16),
                   jax.ShapeDtypeStruct(w_out.shape[1:], BF16),
                   jax.ShapeDtypeStruct(w_ffn_out.shape[1:], BF16)],
        scratch_shapes=[pltpu.VMEM((k, bn), BF16)],
        compiler_params=_params(("arbitrary", "arbitrary")),
        name="in_proj",
    )(x16, w, cos, sin, gate_b3, w_out, w_ffn_out)


def _chunk_pos():
    return lax.broadcasted_iota(jnp.int32, (RET_BLOCK, RET_HEAD_DIM), 0).astype(F32)


def _head_cols(h):
    return slice(h * RET_HEAD_DIM, (h + 1) * RET_HEAD_DIM)


def _ret_kernel(q_ref, k_ref, v_ref, g_ref, dl_ref, o_ref,
                sb_ref, sf_ref, qdb_ref, kdb_ref, qdf_ref, kdf_ref, m_ref, ib_ref, *, nc):
    c = RET_BLOCK
    assert c <= RET_HEAD_DIM
    phase = pl.program_id(0)
    n = pl.program_id(1)

    @pl.when(jnp.logical_and(phase == 0, n == 0))
    def _():
        sb_ref[...] = jnp.zeros_like(sb_ref)
        sf_ref[...] = jnp.zeros_like(sf_ref)
        pos = _chunk_pos()
        ri = lax.broadcasted_iota(jnp.int32, (c, c), 0)
        ci = lax.broadcasted_iota(jnp.int32, (c, c), 1)
        diff = (ri - ci).astype(F32)
        for h in range(RET_HEADS):
            lgf = _log_sigmoid(dl_ref[h])
            lgb = _log_sigmoid(dl_ref[RET_HEADS + h])
            qdb_ref[h] = jnp.exp((c - pos) * lgb)
            kdb_ref[h] = jnp.exp(pos * lgb)
            qdf_ref[h] = jnp.exp((pos + 1.0) * lgf)
            kdf_ref[h] = jnp.exp((c - 1.0 - pos) * lgf)
            m_ref[h] = jnp.exp(jnp.abs(diff) * jnp.where(diff >= 0, lgf[:, :c], lgb[:, :c]))

    @pl.when(phase == 0)
    def _():
        rows = pl.ds(pl.multiple_of((nc - 1 - n) * c, c), c)
        for h in range(RET_HEADS):
            cols = _head_cols(h)
            c_dec = jnp.exp(c * _log_sigmoid(dl_ref[RET_HEADS + h]))
            state = sb_ref[h]
            ib_ref[rows, cols] = _dot(q_ref[:, cols], state.astype(BF16)) * qdb_ref[h]
            vk = (v_ref[:, cols].astype(F32) * kdb_ref[h]).astype(BF16)
            sb_ref[h] = c_dec * state + _dot_tn(k_ref[:, cols], vk)

    @pl.when(phase == 1)
    def _():
        rows = pl.ds(pl.multiple_of(n * c, c), c)
        for h in range(RET_HEADS):
            cols = _head_cols(h)
            q = q_ref[:, cols]
            k = k_ref[:, cols]
            v = v_ref[:, cols]
            p = (_dot_nt(q, k) * m_ref[h]).astype(BF16)
            intra = _dot(p, v)
            state = sf_ref[h]
            inter = _dot(q, state.astype(BF16)) * qdf_ref[h]
            y = intra + inter + ib_ref[rows, cols]

            c_dec = jnp.exp(c * _log_sigmoid(dl_ref[h]))
            vk = (v.astype(F32) * kdf_ref[h]).astype(BF16)
            sf_ref[h] = c_dec * state + _dot_tn(k, vk)

            mu = jnp.mean(y, axis=-1, keepdims=True)
            yc = y - mu
            var = jnp.mean(yc * yc, axis=-1, keepdims=True)
            yn = yc * lax.rsqrt(var + LN_EPS)
            g = g_ref[:, cols].astype(F32)
            o_ref[:, cols] = (g * _sigmoid(g) * yn).astype(BF16)


def _retention(proj16, decay_lanes):
    s = proj16.shape[0]
    c, d, nh = RET_BLOCK, RET_HEAD_DIM, RET_HEADS
    nc = s // c
    w = nh * d
    table = pltpu.VMEM((nh, c, d), F32)

    def chunk(col):
        return pl.BlockSpec((c, w), lambda p, n: (jnp.where(p == 0, nc - 1 - n, n), col))

    def fwd_only(col):
        return pl.BlockSpec((c, w), lambda p, n: (jnp.where(p == 0, 0, n), col))

    return pl.pallas_call(
        functools.partial(_ret_kernel, nc=nc),
        grid=(2, nc),
        in_specs=[chunk(COL_RQ // w), chunk(COL_RK // w), chunk(COL_RV // w), fwd_only(COL_RG // w),
                  pl.BlockSpec((2 * nh, 1, d), lambda p, n: (0, 0, 0))],
        out_specs=fwd_only(0),
        out_shape=jax.ShapeDtypeStruct((s, w), BF16),
        scratch_shapes=[pltpu.VMEM((nh, d, d), F32), pltpu.VMEM((nh, d, d), F32),
                        table, table, table, table, pltpu.VMEM((nh, c, c), F32),
                        pltpu.VMEM((s, w), F32)],
        compiler_params=_params(("arbitrary", "arbitrary")),
        name="retention",
    )(proj16, proj16, proj16, proj16, decay_lanes)


_NA_SPAN = NA_KEY_ROWS - NA_KH
_NA_START_PAIR = ((0, 0), (0, 2), (2, 2))


def _na_first_key_row(case, i):
    return (max(i - NA_KH // 2, 0), i, min(i + NA_KH // 2, _NA_SPAN))[case]


def _na_build_bias(rpb_ref, bias_ref, head):
    w = GRID_W
    n_dc = 2 * NA_KW - 1
    n_dr = 2 * NA_KH - 1
    lane = lax.broadcasted_iota(jnp.int32, (w, 2 * w), 1)
    cq = lax.broadcasted_iota(jnp.int32, (w, 2 * w), 0)
    ck = lane & (w - 1)
    dc = jnp.clip(ck - cq, -(NA_KW - 1), NA_KW - 1) + (NA_KW - 1)
    col_start = jnp.clip(cq - NA_KW // 2, 0, w - NA_KW)
    in_win = (ck >= col_start) & (ck < col_start + NA_KW)
    dc_is = [dc == e for e in range(n_dc)]
    neg = jnp.full((w, 2 * w), NA_NEG, F32)
    rows = []
    for dr in range(n_dr):
        t = jnp.zeros((w, 2 * w), F32)
        base = (head * n_dr + dr) * n_dc
        for e in range(n_dc):
            t = jnp.where(dc_is[e], rpb_ref[base + e], t)
        rows.append(jnp.where(in_win, t * LOG2_E, neg))
    left = lane < w
    for case, off in enumerate((0, _NA_SPAN // 2, _NA_SPAN)):
        for u in range(NA_ROWS_PER_STEP // NA_UNIT_ROWS):
            sp = _NA_START_PAIR[case][u]
            for ii in range(NA_UNIT_ROWS):
                i = u * NA_UNIT_ROWS + ii
                first = _na_first_key_row(case, i)
                assert 2 * sp <= first and first + NA_KH <= 2 * (sp + NA_UNIT_PAIRS)

                def tile(j):
                    if first <= j < first + NA_KH:
                        return rows[j - i + (NA_KH - 1) - off]
                    return None

                for jj in range(NA_UNIT_PAIRS):
                    a, b = tile(2 * (sp + jj)), tile(2 * (sp + jj) + 1)
                    if a is None and b is None:
                        blk = neg
                    else:
                        blk = jnp.where(left, neg if a is None else a, neg if b is None else b)
                    bias_ref[case, u, ii * w:(ii + 1) * w, jj * 2 * w:(jj + 1) * 2 * w] = blk


def _na_kernel(rpb_ref, q_ref, k_ref, v_ref, o_ref, bias_ref, *, rows):
    rb = pl.program_id(1)
    nrb = rows // NA_ROWS_PER_STEP
    d = NA_HEAD_DIM
    uq = NA_UNIT_ROWS * GRID_W
    uk = NA_UNIT_PAIRS * 2 * GRID_W

    @pl.when(rb == 0)
    def _():
        for hh in range(NA_HEADS_PER_STEP):
            _na_build_bias(rpb_ref, bias_ref.at[hh], pl.program_id(0) * NA_HEADS_PER_STEP + hh)

    key_row0 = jnp.clip(rb * NA_ROWS_PER_STEP - NA_KH // 2, 0, rows - NA_KEY_ROWS)
    case = jnp.where(rb == 0, 0, jnp.where(rb == nrb - 1, 2, 1))
    for u in range(NA_ROWS_PER_STEP // NA_UNIT_ROWS):
        first, mid, last = (_NA_START_PAIR[cs][u] for cs in range(3))
        sp = jnp.where(rb == 0, first, jnp.where(rb == nrb - 1, last, mid))
        kstart = pl.multiple_of((key_row0 + 2 * sp) * GRID_W, 2 * GRID_W)
        for hh in range(NA_HEADS_PER_STEP):
            cols = slice(hh * d, (hh + 1) * d)
            kk = k_ref[pl.ds(kstart, uk), cols]
            vv = v_ref[pl.ds(kstart, uk), cols]
            q = q_ref[u * uq:(u + 1) * uq, cols]
            s = _dot_nt(q, kk) + bias_ref[hh, case, u]
            m = jnp.max(s, axis=-1, keepdims=True)
            e = jnp.exp2(s - m)
            l = jnp.sum(e, axis=-1, keepdims=True)
            o = _dot(e.astype(BF16), vv) / l
            o_ref[u * uq:(u + 1) * uq, cols] = o.astype(BF16)


def _neighborhood_attention(proj16, rpb_flat):
    s = proj16.shape[0]
    rows = s // GRID_W
    d = NA_HEAD_DIM * NA_HEADS_PER_STEP
    bq = NA_ROWS_PER_STEP * GRID_W
    nrb = rows // NA_ROWS_PER_STEP
    n_units = NA_ROWS_PER_STEP // NA_UNIT_ROWS
    return pl.pallas_call(
        functools.partial(_na_kernel, rows=rows),
        grid=(NA_HEADS // NA_HEADS_PER_STEP, nrb),
        in_specs=[pl.BlockSpec(memory_space=pltpu.SMEM),
                  pl.BlockSpec((bq, d), lambda h, r: (r, COL_NQ // d + h)),
                  pl.BlockSpec((s, d), lambda h, r: (0, COL_NK // d + h)),
                  pl.BlockSpec((s, d), lambda h, r: (0, COL_NV // d + h))],
        out_specs=pl.BlockSpec((bq, d), lambda h, r: (r, h)),
        out_shape=jax.ShapeDtypeStruct((s, NA_HEADS * NA_HEAD_DIM), BF16),
        scratch_shapes=[pltpu.VMEM((NA_HEADS_PER_STEP, 3, n_units, NA_UNIT_ROWS * GRID_W,
                                    NA_UNIT_PAIRS * 2 * GRID_W), F32)],
        compiler_params=_params(("parallel", "arbitrary")),
        name="natten",
    )(rpb_flat, proj16, proj16, proj16)


def _gelu_tanh(x):
    return x * (0.5 * (1.0 + jnp.tanh(math.sqrt(2.0 / math.pi) * (x + 0.044715 * (x * x * x)))))


def _lru_kernel(x_ref, y_ref, wc_ref, bc_ref, waf_ref, wif_ref, wab_ref, wib_ref, ba_ref, bi_ref,
                lam_ref, o_ref, xpad_ref, xc_ref, hf_ref, a_ref, b_ref, hb_ref, *, seq):
    tt = min(LRU_TILE, seq)
    nt = seq // tt
    ng = tt // 8
    lanes = LRU_BLOCK_DIM
    row = lax.broadcasted_iota(jnp.int32, (8, lanes), 0)

    zeros_halo = jnp.zeros((LRU_HALO, lanes), F32)
    xpad_ref[0:LRU_HALO, :] = zeros_halo
    xpad_ref[LRU_HALO + seq:2 * LRU_HALO + seq, :] = zeros_halo

    def stage(t, carry):
        t0 = pl.multiple_of(t * tt, tt)
        xpad_ref[pl.ds(LRU_HALO + t0, tt), :] = x_ref[pl.ds(t0, tt), :].astype(F32)
        return carry

    lax.fori_loop(0, nt, stage, 0)

    def conv_tile(t0):
        base = LRU_HALO - LRU_CONV // 2
        xc = bc_ref[...]
        for j in range(LRU_CONV):
            xc = xc + xpad_ref[pl.ds(t0 + base + j, tt), :] * wc_ref[j:j + 1, :]
        return xc

    def tile_gates(xc, wa_ref, wi_ref, d):
        xb = xc.astype(BF16)
        t_r = jnp.tanh(_dot(xb, (0.5 * wa_ref[...]).astype(BF16)) + 0.5 * ba_ref[d:d + 1, :])
        t_i = jnp.tanh(_dot(xb, (0.5 * wi_ref[...]).astype(BF16)) + 0.5 * bi_ref[d:d + 1, :])
        neg_log_a = (0.5 * LRU_C * _softplus(-lam_ref[d:d + 1, :])) * (t_r + 1.0)
        a = jnp.exp(-neg_log_a)
        a_ref[...] = a
        gain = _sqrt_nonneg(jnp.tanh(neg_log_a) * (a * a + 1.0))
        b_ref[...] = gain * ((0.5 * t_i + 0.5) * xc)

    def fwd_tile(t, carry):
        t0 = pl.multiple_of(t * tt, tt)
        xc = conv_tile(t0)
        xc_ref[pl.ds(t0, tt), :] = xc
        tile_gates(xc, waf_ref, wif_ref, 0)

        def group(gi, c):
            r0 = pl.multiple_of(gi * 8, 8)
            a = a_ref[pl.ds(r0, 8), :]
            b = b_ref[pl.ds(r0, 8), :]
            for sh in (1, 2, 4):
                keep = row >= sh
                a_sh = jnp.where(keep, pltpu.roll(a, sh, 0), 1.0)
                b_sh = jnp.where(keep, pltpu.roll(b, sh, 0), 0.0)
                b = a * b_sh + b
                a = a * a_sh
            h = a * c + b
            hf_ref[pl.ds(t0 + r0, 8), :] = h
            return jnp.broadcast_to(h[7:8, :], (8, lanes))

        return lax.fori_loop(0, ng, group, carry, unroll=8)

    lax.fori_loop(0, nt, fwd_tile, jnp.zeros((8, lanes), F32))

    def bwd_tile(ti, carry):
        t0 = pl.multiple_of((nt - 1 - ti) * tt, tt)
        tile_gates(xc_ref[pl.ds(t0, tt), :], wab_ref, wib_ref, 1)

        def group(gj, c):
            r0 = pl.multiple_of((ng - 1 - gj) * 8, 8)
            a = a_ref[pl.ds(r0, 8), :]
            b = b_ref[pl.ds(r0, 8), :]
            for sh in (1, 2, 4):
                keep = row < 8 - sh
                a_sh = jnp.where(keep, pltpu.roll(a, 8 - sh, 0), 1.0)
                b_sh = jnp.where(keep, pltpu.roll(b, 8 - sh, 0), 0.0)
                b = a * b_sh + b
                a = a * a_sh
            h = a * c + b
            hb_ref[pl.ds(r0, 8), :] = h
            return jnp.broadcast_to(h[0:1, :], (8, lanes))

        carry = lax.fori_loop(0, ng, group, carry, unroll=8)
        h = hf_ref[pl.ds(t0, tt), :] + hb_ref[...]
        y = y_ref[pl.ds(t0, tt), :].astype(F32)
        o_ref[pl.ds(t0, tt), :] = (h * _gelu_tanh(y)).astype(BF16)
        return carry

    lax.fori_loop(0, nt, bwd_tile, jnp.zeros((8, lanes), F32))


def _rglru(proj16, w_conv, b_conv, wa, ba, wi, bi, lam):
    s = proj16.shape[0]
    bd = LRU_BLOCK_DIM
    tt = min(LRU_TILE, s)
    col = lambda off: pl.BlockSpec((s, bd), lambda g: (0, off // bd + g))
    gate_w = lambda d: pl.BlockSpec((None, None, bd, bd), lambda g: (d, g, 0, 0))
    vec2 = pl.BlockSpec((2, bd), lambda g: (0, g))
    return pl.pallas_call(
        functools.partial(_lru_kernel, seq=s),
        grid=(LRU_BLOCKS,),
        in_specs=[col(COL_LX), col(COL_LY),
                  pl.BlockSpec((LRU_CONV, bd), lambda g: (0, g)),
                  pl.BlockSpec((1, bd), lambda g: (0, g)),
                  gate_w(0), gate_w(0), gate_w(1), gate_w(1),
                  vec2, vec2, vec2],
        out_specs=pl.BlockSpec((s, bd), lambda g: (0, g)),
        out_shape=jax.ShapeDtypeStruct((s, LRU_WIDTH), BF16),
        scratch_shapes=[pltpu.VMEM((s + 2 * LRU_HALO, bd), F32),
                        pltpu.VMEM((s, bd), F32),
                        pltpu.VMEM((s, bd), F32),
                        pltpu.VMEM((tt, bd), F32),
                        pltpu.VMEM((tt, bd), F32),
                        pltpu.VMEM((tt, bd), F32)],
        compiler_params=_params(("parallel",)),
        name="rglru",
    )(proj16, proj16, w_conv, b_conv, wa, wi, wa, wi, ba, bi, lam)


def _merge_kernel(r_ref, n_ref, l_ref, wr_ref, wn_ref, wl_ref, gr_ref, gn_ref, gl_ref, o_ref,
                  wr16_ref, wn16_ref, wl16_ref):
    @pl.when(pl.program_id(1) == 0)
    def _():
        for w_ref, w16_ref in ((wr_ref, wr16_ref), (wn_ref, wn16_ref), (wl_ref, wl16_ref)):
            w16_ref[...] = w_ref[...].astype(BF16)

    acc = None
    for x_ref, w16_ref, g_ref in ((r_ref, wr16_ref, gr_ref), (n_ref, wn16_ref, gn_ref),
                                  (l_ref, wl16_ref, gl_ref)):
        term = g_ref[...].astype(F32) * _dot(x_ref[...], w16_ref[...])
        acc = term if acc is None else acc + term
    o_ref[...] = acc.astype(BF16)


def _merge(ret16, na16, lru16, proj16, w_branch, layer, bm=512, bn=1024):
    s = ret16.shape[0]
    d = D_MODEL
    kw = BRANCH_WIDTH
    xin = pl.BlockSpec((bm, kw), lambda j, i: (i, 0))
    wsp = lambda b: pl.BlockSpec((None, None, kw, bn), lambda j, i: (layer, b, 0, j))
    gsp = lambda b: pl.BlockSpec((bm, bn), lambda j, i: (i, (COL_GATE + b * d) // bn + j))
    return pl.pallas_call(
        _merge_kernel,
        grid=(d // bn, s // bm),
        in_specs=[xin, xin, xin, wsp(0), wsp(1), wsp(2), gsp(0), gsp(1), gsp(2)],
        out_specs=pl.BlockSpec((bm, bn), lambda j, i: (i, j)),
        out_shape=jax.ShapeDtypeStruct((s, d), BF16),
        scratch_shapes=[pltpu.VMEM((kw, bn), BF16)] * N_BRANCH,
        compiler_params=_params(("parallel", "arbitrary")),
        name="merge",
    )(ret16, na16, lru16, w_branch, w_branch, w_branch, proj16, proj16, proj16)


def _mm_res_kernel(x_ref, w_ref, h_ref, o_ref):
    o_ref[...] = DEEPNORM_ALPHA * h_ref[...] + _dot(x_ref[...], w_ref[...])


def _mm_res_ln_kernel(x_ref, w_ref, r_ref, g_ref, b_ref, o32_ref, o16_ref, *, res_scale):
    bm = o32_ref.shape[0]
    rg = bm // LN_ROW_GROUPS
    for rows in (slice(r * rg, (r + 1) * rg) for r in range(LN_ROW_GROUPS)):
        res = r_ref[rows, :] if res_scale == 1.0 else res_scale * r_ref[rows, :]
        y = _layer_norm_rows(res + _dot(x_ref[rows, :], w_ref[...]), g_ref[...], b_ref[...])
        o32_ref[rows, :] = y
        o16_ref[rows, :] = y.astype(BF16)


def _kblock_specs(bm, bk, kblock):
    return [pl.BlockSpec((bm, bk), lambda i: (i, kblock)),
            pl.BlockSpec((bk, D_MODEL), lambda i: (kblock, 0))]


def _mm_res(x16, w16, h32, name, bm, bk, kblock):
    s = x16.shape[0]
    row = pl.BlockSpec((bm, D_MODEL), lambda i: (i, 0))
    return pl.pallas_call(
        _mm_res_kernel,
        grid=(s // bm,),
        in_specs=_kblock_specs(bm, bk, kblock) + [row],
        out_specs=row,
        out_shape=jax.ShapeDtypeStruct((s, D_MODEL), F32),
        compiler_params=_params(("parallel",)),
        name=name,
    )(x16, w16, h32)


def _mm_res_ln(x16, w16, layer, res32, res_scale, g, b, name, bm, bk, kblock=0):
    s = x16.shape[0]
    d = D_MODEL
    row = pl.BlockSpec((bm, d), lambda i: (i, 0))
    vec = pl.BlockSpec((None, 1, d), lambda i: (layer, 0, 0))
    return pl.pallas_call(
        functools.partial(_mm_res_ln_kernel, res_scale=res_scale),
        grid=(s // bm,),
        in_specs=_kblock_specs(bm, bk, kblock) + [row, vec, vec],
        out_specs=[row, row],
        out_shape=[jax.ShapeDtypeStruct((s, d), F32), jax.ShapeDtypeStruct((s, d), BF16)],
        compiler_params=_params(("parallel",)),
        name=name,
    )(x16, w16, res32, g, b)


def _ffn_in_kernel(x_ref, wg_ref, wv_ref, o_ref, wg16_ref, wv16_ref):
    @pl.when(pl.program_id(1) == 0)
    def _():
        wg16_ref[...] = wg_ref[...].astype(BF16)
        wv16_ref[...] = wv_ref[...].astype(BF16)

    bm = o_ref.shape[0]
    rg = min(bm, IN_PROJ_ROW_GROUP)
    for r in range(bm // rg):
        rows = slice(r * rg, (r + 1) * rg)
        x = x_ref[rows, :]
        gate = _dot(x, wg16_ref[...])
        val = _dot(x, wv16_ref[...])
        o_ref[rows, :] = (gate * _sigmoid(gate) * val).astype(BF16)


def _ffn_in(h16, w, layer, bm=2048, bn=512):
    s, k = h16.shape
    bm = min(bm, s)
    nb = D_FF // bn
    return pl.pallas_call(
        _ffn_in_kernel,
        grid=(nb, s // bm),
        in_specs=[pl.BlockSpec((bm, k), lambda j, i: (i, 0)),
                  pl.BlockSpec((None, k, bn), lambda j, i: (layer, 0, j)),
                  pl.BlockSpec((None, k, bn), lambda j, i: (layer, 0, nb + j))],
        out_specs=pl.BlockSpec((bm, bn), lambda j, i: (i, j)),
        out_shape=jax.ShapeDtypeStruct((s, D_FF), BF16),
        scratch_shapes=[pltpu.VMEM((k, bn), BF16), pltpu.VMEM((k, bn), BF16)],
        compiler_params=_params(("parallel", "arbitrary")),
        name="ffn_in",
    )(h16, w, w)


def _rotary_tables(seq):
    half = RET_HEAD_DIM // 2
    inv = ROPE_BASE ** (-jnp.arange(half, dtype=F32) / half)
    ang = jnp.arange(seq, dtype=F32)[:, None] * inv[None, :]
    return jnp.cos(ang), jnp.sin(ang)


def _mixer_branches(proj16, ret_decay, w_conv, b_conv, lru_wa, lru_ba, lru_wi, lru_bi,
                    lru_lambda, na_rpb):
    decay_lanes = jnp.broadcast_to(ret_decay.reshape(2 * RET_HEADS, 1, 1),
                                   (2 * RET_HEADS, 1, RET_HEAD_DIM))
    ret16 = _retention(proj16, decay_lanes)
    na16 = _neighborhood_attention(proj16, na_rpb.reshape(-1))
    lru16 = _rglru(proj16, w_conv, b_conv.reshape(1, LRU_WIDTH), lru_wa, lru_ba, lru_wi, lru_bi,
                   lru_lambda)
    return ret16, na16, lru16


def kernel(x, ln_in_g, ln_in_b, w_in, gate_b, ret_decay, w_conv, b_conv, lru_wa, lru_ba, lru_wi, lru_bi, lru_lambda, na_rpb, w_branch, w_out, ln1_g, ln1_b, w_ffn_in, w_ffn_out, ln2_g, ln2_b):
    batch, seq, d = x.shape
    assert batch == 1 and d == D_MODEL
    gate_b3 = gate_b.reshape(DEPTH, 1, N_BRANCH * d)
    vec3 = lambda v: v.reshape(DEPTH, 1, d)
    ln1_g3, ln1_b3, ln2_g3, ln2_b3 = vec3(ln1_g), vec3(ln1_b), vec3(ln2_g), vec3(ln2_b)
    cos, sin = _rotary_tables(seq)

    h32, h16 = _ln(x.reshape(seq, d), ln_in_g, ln_in_b)
    for l in range(DEPTH):
        proj16, w_out16, w_ffn_out16 = _in_proj(h16, w_in, l, cos, sin, gate_b3, w_out, w_ffn_out)
        ret16, na16, lru16 = _mixer_branches(proj16, ret_decay[l], w_conv[l], b_conv[l],
                                             lru_wa[l], lru_ba[l], lru_wi[l], lru_bi[l],
                                             lru_lambda[l], na_rpb[l])
        merged16 = _merge(ret16, na16, lru16, proj16, w_branch, l)
        h32, h16 = _mm_res_ln(merged16, w_out16, l, h32, DEEPNORM_ALPHA, ln1_g3, ln1_b3, "out_proj",
                              bm=512, bk=D_MODEL)
        act16 = _ffn_in(h16, w_ffn_in, l)
        part = _mm_res(act16, w_ffn_out16, h32, "ffn_out_a", bm=512, bk=D_FF // 2, kblock=0)
        h32, h16 = _mm_res_ln(act16, w_ffn_out16, l, part, 1.0, ln2_g3, ln2_b3, "ffn_out_b",
                              bm=512, bk=D_FF // 2, kblock=1)
    return h32.reshape(batch, seq, d)
```
